```python
import math
import jax, jax.numpy as jnp
from jax import lax
import numpy as np

D_MODEL = 1024
BATCH = 1
SEQ = 16384
DEPTH = 1
DEC_BATCH = 8
DEC_SEQ = 8192
PAST_LEN = 128

LRU_WIDTH = D_MODEL
LRU_BLOCK = 64
LRU_BLOCKS = LRU_WIDTH // LRU_BLOCK
LRU_C = 8.0
CONV_W = 4
CONV_LEFT = 2
N_HEADS = 8
HEAD_DIM = 64
V_DIM = 2 * HEAD_DIM
QK_WIDTH = N_HEADS * 2 * HEAD_DIM
ATTN_WIDTH = N_HEADS * V_DIM
Q_BLOCK = 128
ROPE_THETA = 10000.0
N_EXPERTS = 32
TOP_K = 4
D_EXPERT = D_MODEL
SWIGLU_LIMIT = 7.0
SWIGLU_ALPHA = 1.702
ROWS_PER_BLOCK = 512
PLE_DIM = 256
EPS = 1e-6
SPLITS = [LRU_WIDTH, 2 * LRU_WIDTH, 2 * LRU_WIDTH + QK_WIDTH, 2 * LRU_WIDTH + 2 * QK_WIDTH,
          2 * LRU_WIDTH + 2 * QK_WIDTH + ATTN_WIDTH, 2 * LRU_WIDTH + 2 * QK_WIDTH + ATTN_WIDTH + D_MODEL]
D_IN = 2 * LRU_WIDTH + 2 * QK_WIDTH + ATTN_WIDTH + 2 * D_MODEL

kernel_name = "hybrid_rglru_diffattn_moe_encoder"


def rmsnorm(x, g):
    xf = x.astype(jnp.float32)
    y = xf * lax.rsqrt(jnp.mean(xf * xf, axis=-1, keepdims=True) + EPS)
    return (y * g.astype(jnp.float32)).astype(x.dtype)


def centred_dwconv(x, w, b):
    S = x.shape[1]
    xp = jnp.pad(x, ((0, 0), (CONV_LEFT, CONV_W - 1 - CONV_LEFT), (0, 0)))
    y = b
    for j in range(CONV_W):
        y = y + xp[:, j:j + S] * w[j]
    return y


def _lin_combine(c1, c2):
    a1, b1 = c1
    a2, b2 = c2
    return a1 * a2, a2 * b1 + b2


def rg_lru(x, wa, ba, wx, bx, lam, reverse):
    B, S, C = x.shape
    xb = x.reshape(B, S, LRU_BLOCKS, LRU_BLOCK)
    r = jax.nn.sigmoid(jnp.einsum('bsnd,nde->bsne', xb, wa).reshape(B, S, C) + ba)
    i = jax.nn.sigmoid(jnp.einsum('bsnd,nde->bsne', xb, wx).reshape(B, S, C) + bx)
    log_a = -LRU_C * r.astype(jnp.float32) * jax.nn.softplus(-lam.astype(jnp.float32))
    a = jnp.exp(log_a)
    bterm = jnp.sqrt(-jnp.expm1(2.0 * log_a)) * (i * x).astype(jnp.float32)
    _, h = lax.associative_scan(_lin_combine, (a, bterm), axis=1, reverse=reverse)
    return h.astype(x.dtype)


def rope_tables(S):
    pos = jnp.arange(S, dtype=jnp.float32)
    inv = ROPE_THETA ** (-jnp.arange(0, HEAD_DIM, 2, dtype=jnp.float32) / HEAD_DIM)
    ang = pos[:, None] * inv[None, :]
    ang = jnp.concatenate([ang, ang], axis=-1)
    return jnp.cos(ang), jnp.sin(ang)


def apply_rope(x, cos, sin):
    c = cos[None, :, None, None, :].astype(x.dtype)
    s = sin[None, :, None, None, :].astype(x.dtype)
    x1, x2 = jnp.split(x, 2, axis=-1)
    rot = jnp.concatenate([-x2, x1], axis=-1)
    return x * c + rot * s


def diff_attention(q, k, v, lam):
    B, S = q.shape[0], q.shape[1]
    nq = S // Q_BLOCK
    qb = (q * (HEAD_DIM ** -0.5)).reshape(B, nq, Q_BLOCK, N_HEADS, 2, HEAD_DIM)
    qb = jnp.moveaxis(qb, 1, 0)

    def block(qi):
        s = jnp.einsum('bqhcd,bkhcd->bhcqk', qi, k).astype(jnp.float32)
        p = jax.nn.softmax(s, axis=-1)
        w = (p[:, :, 0] - lam * p[:, :, 1]).astype(v.dtype)
        return jnp.einsum('bhqk,bkhe->bqhe', w, v)

    o = lax.map(block, qb)
    return jnp.moveaxis(o, 0, 1).reshape(B, S, N_HEADS, V_DIM)


def clamped_swiglu(h):
    x_glu = jnp.minimum(h[..., ::2], SWIGLU_LIMIT)
    x_lin = jnp.clip(h[..., 1::2], -SWIGLU_LIMIT, SWIGLU_LIMIT)
    return x_glu * jax.nn.sigmoid(SWIGLU_ALPHA * x_glu) * (x_lin + 1.0)


def moe_ffn(x, w_router, b_router, w1, b1, w2, b2):
    B, S, D = x.shape
    T = B * S
    TK = T * TOP_K
    xt = x.reshape(T, D)
    logits = (xt @ w_router + b_router).astype(jnp.float32)
    top_val, top_idx = lax.top_k(logits, TOP_K)
    gates = jax.nn.softmax(top_val, axis=-1).astype(x.dtype)
    flat_e = top_idx.reshape(TK)
    flat_tok = jnp.arange(TK, dtype=jnp.int32) // TOP_K
    order = jnp.argsort(flat_e)
    sorted_e = flat_e[order]
    counts = jnp.bincount(flat_e, length=N_EXPERTS)
    padded = ((counts + ROWS_PER_BLOCK - 1) // ROWS_PER_BLOCK) * ROWS_PER_BLOCK
    pad_end = jnp.cumsum(padded)
    pad_start = pad_end - padded
    start = jnp.cumsum(counts) - counts
    dest = pad_start[sorted_e] + jnp.arange(TK, dtype=jnp.int32) - start[sorted_e]
    n_blocks = -(-TK // ROWS_PER_BLOCK) + N_EXPERTS
    n_rows = n_blocks * ROWS_PER_BLOCK
    row_tok = jnp.zeros((n_rows,), jnp.int32).at[dest].set(flat_tok[order])
    row_gate = jnp.zeros((n_rows,), x.dtype).at[dest].set(gates.reshape(TK)[order])
    block_e = jnp.minimum(
        jnp.searchsorted(pad_end, jnp.arange(n_blocks, dtype=pad_end.dtype) * ROWS_PER_BLOCK, side='right'),
        N_EXPERTS - 1)
    xs = xt[row_tok].reshape(n_blocks, ROWS_PER_BLOCK, D)
    gs = row_gate.reshape(n_blocks, ROWS_PER_BLOCK, 1)

    def expert_block(args):
        xb, gb, e = args
        hb = clamped_swiglu(xb @ w1[e] + b1[e])
        return (hb @ w2[e] + b2[e]) * gb

    ys = lax.map(expert_block, (xs, gs, block_e)).reshape(n_rows, D)
    y = jax.ops.segment_sum(ys, row_tok, num_segments=T)
    return y.reshape(B, S, D)


def trunk(x, p, g_mix, w_in, conv_w, conv_b, lru_wa, lru_ba, lru_wx, lru_bx, lru_lambda,
          lam_q1, lam_k1, lam_q2, lam_k2, subln_g, w_out, g_ffn, w_router, b_router,
          w1, b1, w2, b2, g_pl, w_pg, w_pe, g_final):
    B, S, _ = x.shape
    cos, sin = rope_tables(S)
    h = x
    for l in range(DEPTH):
        lam_init = 0.8 - 0.6 * math.exp(-0.3 * l)
        u = rmsnorm(h, g_mix[l])
        z = u @ w_in[l]
        xr, gr, q, k, v, ga, gb = jnp.split(z, SPLITS, axis=-1)
        xc = centred_dwconv(xr, conv_w[l], conv_b[l])
        h_f = rg_lru(xc, lru_wa[l, 0], lru_ba[l, 0], lru_wx[l, 0], lru_bx[l, 0], lru_lambda[l, 0], False)
        h_b = rg_lru(xc, lru_wa[l, 1], lru_ba[l, 1], lru_wx[l, 1], lru_bx[l, 1], lru_lambda[l, 1], True)
        y_a = jax.nn.gelu(gr) * (h_f + h_b)
        q = apply_rope(q.reshape(B, S, N_HEADS, 2, HEAD_DIM), cos, sin)
        k = apply_rope(k.reshape(B, S, N_HEADS, 2, HEAD_DIM), cos, sin)
        v = v.reshape(B, S, N_HEADS, V_DIM)
        lam = (jnp.exp(jnp.sum(lam_q1[l].astype(jnp.float32) * lam_k1[l].astype(jnp.float32)))
               - jnp.exp(jnp.sum(lam_q2[l].astype(jnp.float32) * lam_k2[l].astype(jnp.float32)))
               + lam_init)
        o = diff_attention(q, k, v, lam)
        y_b = (rmsnorm(o, subln_g[l]) * (1.0 - lam_init)).reshape(B, S, ATTN_WIDTH)
        mixed = jax.nn.sigmoid(ga) * y_a + jax.nn.sigmoid(gb) * y_b
        h = h + mixed @ w_out[l]
        h = h + moe_ffn(rmsnorm(h, g_ffn[l]), w_router[l], b_router[l], w1[l], b1[l], w2[l], b2[l])
        gate = jax.nn.sigmoid(rmsnorm(h, g_pl[l]) @ w_pg[l])
        h = h + gate * (p[l] @ w_pe[l])
    return rmsnorm(h, g_final)


def setup_inputs(seed: int = 0) -> dict:
    key = jax.random.key(seed)
    ks = jax.random.split(key, 32)
    f32 = jnp.float32
    nrm = lambda k, shape, s: jax.random.normal(k, shape, f32) * s
    a0 = jax.random.uniform(ks[12], (DEPTH, 2, LRU_WIDTH), f32, 0.9, 0.999)
    return {
        "x_prompt": nrm(ks[0], (BATCH, SEQ, D_MODEL), 1.0),
        "x_sample": nrm(ks[1], (DEC_BATCH, DEC_SEQ, D_MODEL), 1.0),
        "p_prompt": nrm(ks[2], (DEPTH, BATCH, SEQ, PLE_DIM), 1.0),
        "p_sample": nrm(ks[3], (DEPTH, DEC_BATCH, DEC_SEQ, PLE_DIM), 1.0),
        "g_mix": 1.0 + nrm(ks[4], (DEPTH, D_MODEL), 0.02),
        "w_in": nrm(ks[5], (DEPTH, D_MODEL, D_IN), D_MODEL ** -0.5),
        "conv_w": nrm(ks[6], (DEPTH, CONV_W, LRU_WIDTH), CONV_W ** -0.5),
        "conv_b": nrm(ks[7], (DEPTH, LRU_WIDTH), 0.01),
        "lru_wa": nrm(ks[8], (DEPTH, 2, LRU_BLOCKS, LRU_BLOCK, LRU_BLOCK), LRU_BLOCK ** -0.5),
        "lru_ba": nrm(ks[9], (DEPTH, 2, LRU_WIDTH), 0.01),
        "lru_wx": nrm(ks[10], (DEPTH, 2, LRU_BLOCKS, LRU_BLOCK, LRU_BLOCK), LRU_BLOCK ** -0.5),
        "lru_bx": nrm(ks[11], (DEPTH, 2, LRU_WIDTH), 0.01),
        "lru_lambda": jnp.log(a0) - jnp.log1p(-a0),
        "lam_q1": nrm(ks[13], (DEPTH, HEAD_DIM), 0.1),
        "lam_k1": nrm(ks[14], (DEPTH, HEAD_DIM), 0.1),
        "lam_q2": nrm(ks[15], (DEPTH, HEAD_DIM), 0.1),
        "lam_k2": nrm(ks[16], (DEPTH, HEAD_DIM), 0.1),
        "subln_g": 1.0 + nrm(ks[17], (DEPTH, V_DIM), 0.02),
        "w_out": nrm(ks[18], (DEPTH, D_MODEL, D_MODEL), D_MODEL ** -0.5),
        "g_ffn": 1.0 + nrm(ks[19], (DEPTH, D_MODEL), 0.02),
        "w_router": nrm(ks[20], (DEPTH, D_MODEL, N_EXPERTS), D_MODEL ** -0.5),
        "b_router": nrm(ks[21], (DEPTH, N_EXPERTS), 0.01),
        "w1": nrm(ks[22], (DEPTH, N_EXPERTS, D_MODEL, 2 * D_EXPERT), D_MODEL ** -0.5),
        "b1": nrm(ks[23], (DEPTH, N_EXPERTS, 2 * D_EXPERT), 0.01),
        "w2": nrm(ks[24], (DEPTH, N_EXPERTS, D_EXPERT, D_MODEL), D_EXPERT ** -0.5),
        "b2": nrm(ks[25], (DEPTH, N_EXPERTS, D_MODEL), 0.01),
        "g_pl": 1.0 + nrm(ks[26], (DEPTH, D_MODEL), 0.02),
        "w_pg": nrm(ks[27], (DEPTH, D_MODEL, D_MODEL), D_MODEL ** -0.5),
        "w_pe": nrm(ks[28], (DEPTH, PLE_DIM, D_MODEL), PLE_DIM ** -0.5),
        "g_final": 1.0 + nrm(ks[29], (D_MODEL,), 0.02),
    }


def reference(x_prompt, x_sample, p_prompt, p_sample, g_mix, w_in, conv_w, conv_b, lru_wa, lru_ba,
              lru_wx, lru_bx, lru_lambda, lam_q1, lam_k1, lam_q2, lam_k2, subln_g, w_out, g_ffn,
              w_router, b_router, w1, b1, w2, b2, g_pl, w_pg, w_pe, g_final):
    weights = (g_mix, w_in, conv_w, conv_b, lru_wa, lru_ba, lru_wx, lru_bx, lru_lambda,
               lam_q1, lam_k1, lam_q2, lam_k2, subln_g, w_out, g_ffn, w_router, b_router,
               w1, b1, w2, b2, g_pl, w_pg, w_pe, g_final)
    y_prompt = trunk(x_prompt, p_prompt, *weights)
    y_sample = trunk(x_sample, p_sample, *weights)
    return (y_prompt, y_sample)
```

```python
import functools
import math

import jax
import jax.numpy as jnp
from jax import lax
from jax.experimental import pallas as pl
from jax.experimental.pallas import tpu as pltpu

F32 = jnp.float32
BF16 = jnp.bfloat16

D_MODEL = 1024
LRU_BLOCK = 64
LRU_C = 8.0
CONV_W = 4
CONV_LEFT = 2
N_HEADS = 8
HEAD_DIM = 64
V_DIM = 128
ROPE_THETA = 10000.0
N_EXPERTS = 32
TOP_K = 4
SWIGLU_LIMIT = 7.0
SWIGLU_ALPHA = 1.702
ROWS_PER_BLOCK = 512
PLE_DIM = 256
EPS = 1e-6
LAM_INIT = 0.8 - 0.6 * math.exp(-0.3 * 0)
LOG2E = 1.4426950408889634

LANES = 128
SUBLANES = 8
BF16_ROWS = 16
LRU_GROUP = 256
VMEM_LIMIT = 56 * 1024 * 1024


def _tile(n, pref):
    t = min(n, pref)
    while n % t:
        t //= 2
    return t


def _cparams(sem):
    return pltpu.CompilerParams(dimension_semantics=sem, vmem_limit_bytes=VMEM_LIMIT)


def _rms(x, g):
    return x * lax.rsqrt(jnp.mean(x * x, axis=-1, keepdims=True) + EPS) * g


def _sigmoid(x):
    return 1.0 / (1.0 + jnp.exp(-x))


def _inproj_kernel(x_ref, g_ref, cos_ref, sin_ref, w_ref, wvt_ref,
                   xr_ref, gr_ref, q_ref, k_ref, vt_ref, ga_ref, gb_ref):
    u = _rms(x_ref[...], g_ref[...]).astype(BF16)

    def proj(seg):
        return jnp.dot(u, w_ref[:, seg * D_MODEL:(seg + 1) * D_MODEL], preferred_element_type=F32)

    reps = D_MODEL // LANES
    cos = jnp.concatenate([cos_ref[...]] * reps, axis=1)
    sin = jnp.concatenate([sin_ref[...]] * reps, axis=1)
    lane = lax.broadcasted_iota(jnp.int32, cos.shape, 1)
    first_half = (lane % HEAD_DIM) < (HEAD_DIM // 2)

    def rope(z):
        swapped = jnp.where(first_half,
                            pltpu.roll(z, D_MODEL - HEAD_DIM // 2, 1),
                            pltpu.roll(z, HEAD_DIM // 2, 1))
        return z * cos + swapped * sin

    xr_ref[...] = proj(0).astype(BF16)
    gr_ref[...] = proj(1).astype(BF16)
    q_ref[...] = (rope(proj(2)) * (HEAD_DIM ** -0.5 * LOG2E)).astype(BF16)
    k_ref[...] = rope(proj(3)).astype(BF16)
    ga_ref[...] = proj(4).astype(BF16)
    gb_ref[...] = proj(5).astype(BF16)
    vt = lax.dot_general(wvt_ref[...], u, (((1,), (1,)), ((), ())), preferred_element_type=F32)
    vt_ref[...] = vt.astype(BF16)


def _inproj(x2d, g_mix, cos_t, sin_t, w_rest, w_vt, seq):
    T = x2d.shape[0]
    tm = _tile(seq, 512)
    n_pos = seq // tm
    tok = pl.BlockSpec((tm, D_MODEL), lambda i: (i, 0))
    pos = pl.BlockSpec((tm, LANES), lambda i: (i % n_pos, 0))
    const = lambda shape: pl.BlockSpec(shape, lambda i: (0,) * len(shape), pipeline_mode=pl.Buffered(1))
    outs = [jax.ShapeDtypeStruct((T, D_MODEL), BF16)] * 4
    out_shape = (outs[0], outs[1], outs[2], outs[3],
                 jax.ShapeDtypeStruct((D_MODEL, T), BF16), outs[0], outs[0])
    return pl.pallas_call(
        _inproj_kernel,
        grid=(T // tm,),
        in_specs=[tok, const((1, D_MODEL)), pos, pos,
                  const(w_rest.shape), const(w_vt.shape)],
        out_specs=(tok, tok, tok, tok, pl.BlockSpec((D_MODEL, tm), lambda i: (0, i)), tok, tok),
        out_shape=out_shape,
        compiler_params=_cparams(("parallel",)),
        name="inproj",
    )(x2d, g_mix, cos_t, sin_t, w_rest, w_vt)


def _lru_kernel(xm_ref, xp_ref, xn_ref, cw_ref, cb_ref, wg_ref, bg_ref, lam_ref,
                h_ref, a_s, b_s, carry_s, *, reverse, tm, n_tiles):
    i = pl.program_id(1)
    ti = (n_tiles - 1 - i) if reverse else i
    C = D_MODEL

    @pl.when(i == 0)
    def _():
        carry_s[...] = jnp.zeros_like(carry_s)

    xm = xm_ref[0].astype(F32)
    xp = xp_ref[0].astype(F32) * (ti > 0).astype(F32)
    xn = xn_ref[0].astype(F32) * (ti < n_tiles - 1).astype(F32)
    big = jnp.concatenate([xp, xm, xn], axis=0)
    xc = jnp.broadcast_to(cb_ref[...], (tm, C))
    for j in range(CONV_W):
        off = BF16_ROWS + j - CONV_LEFT
        xc = xc + big[off:off + tm, :] * cw_ref[j:j + 1, :]

    xcb = xc.astype(BF16)
    r_parts, i_parts = [], []
    for g in range(C // LRU_GROUP):
        z = jnp.dot(xcb[:, g * LRU_GROUP:(g + 1) * LRU_GROUP], wg_ref[g], preferred_element_type=F32)
        z = z + bg_ref[g]
        r_parts.append(z[:, :LRU_GROUP])
        i_parts.append(z[:, LRU_GROUP:])
    r = _sigmoid(jnp.concatenate(r_parts, axis=1))
    ig = _sigmoid(jnp.concatenate(i_parts, axis=1))

    neg_lam = -lam_ref[...]
    softplus = jnp.maximum(neg_lam, 0.0) + jnp.log1p(jnp.exp(-jnp.abs(neg_lam)))
    a = jnp.exp((-LRU_C * softplus) * r)
    a_s[...] = a
    b_s[...] = jnp.sqrt(1.0 - a * a) * (ig * xc)

    n_slab = tm // SUBLANES
    row = lax.broadcasted_iota(jnp.int32, (SUBLANES, C), 0)

    def body(s, carry):
        sl = (n_slab - 1 - s) if reverse else s
        r0 = pl.multiple_of(sl * SUBLANES, SUBLANES)
        a8 = a_s[pl.ds(r0, SUBLANES), :]
        b8 = b_s[pl.ds(r0, SUBLANES), :]
        for d in (1, 2, 4):
            if reverse:
                valid = row < SUBLANES - d
                shift = SUBLANES - d
            else:
                valid = row >= d
                shift = d
            a_sh = jnp.where(valid, pltpu.roll(a8, shift, 0), 1.0)
            b_sh = jnp.where(valid, pltpu.roll(b8, shift, 0), 0.0)
            b8 = a8 * b_sh + b8
            a8 = a8 * a_sh
        h8 = b8 + a8 * carry
        b_s[pl.ds(r0, SUBLANES), :] = h8
        edge = h8[0:1, :] if reverse else h8[SUBLANES - 1:SUBLANES, :]
        return jnp.broadcast_to(edge, (SUBLANES, C))

    carry_s[...] = lax.fori_loop(0, n_slab, body, carry_s[...])
    h_ref[0] = b_s[...].astype(BF16)


def _lru(xr3, conv_w, conv_b, wg, bg, lam, reverse):
    B, S, C = xr3.shape
    tm = _tile(S, 256)
    n_tiles = S // tm
    hb = tm // BF16_ROWS
    n_halo = S // BF16_ROWS
    tidx = (lambda i: n_tiles - 1 - i) if reverse else (lambda i: i)
    const = lambda shape: pl.BlockSpec(shape, lambda b, i: (0,) * len(shape))
    kern = functools.partial(_lru_kernel, reverse=reverse, tm=tm, n_tiles=n_tiles)
    return pl.pallas_call(
        kern,
        grid=(B, n_tiles),
        in_specs=[
            pl.BlockSpec((1, tm, C), lambda b, i: (b, tidx(i), 0)),
            pl.BlockSpec((1, BF16_ROWS, C), lambda b, i: (b, jnp.maximum(tidx(i) * hb - 1, 0), 0)),
            pl.BlockSpec((1, BF16_ROWS, C), lambda b, i: (b, jnp.minimum((tidx(i) + 1) * hb, n_halo - 1), 0)),
            const(conv_w.shape), const(conv_b.shape), const(wg.shape), const(bg.shape), const(lam.shape),
        ],
        out_specs=pl.BlockSpec((1, tm, C), lambda b, i: (b, tidx(i), 0)),
        out_shape=jax.ShapeDtypeStruct((B, S, C), BF16),
        scratch_shapes=[pltpu.VMEM((tm, C), F32), pltpu.VMEM((tm, C), F32), pltpu.VMEM((SUBLANES, C), F32)],
        compiler_params=_cparams(("parallel", "arbitrary")),
        name="lru_bwd" if reverse else "lru_fwd",
    )(xr3, xr3, xr3, conv_w, conv_b, wg, bg, lam)


def _attn_kernel(lq1_ref, lk1_ref, lq2_ref, lk2_ref, q_ref, k_ref, vt_ref, g_ref,
                 o_ref, qz_s, m_s, l_s, acc_s, *, tq, tk, seq):
    q = q_ref[0]
    lane = lax.broadcasted_iota(jnp.int32, q.shape, 1)
    zero = jnp.zeros_like(q)
    qz_s[0:tq, :] = jnp.where(lane < HEAD_DIM, q, zero)
    qz_s[tq:2 * tq, :] = jnp.where(lane >= HEAD_DIM, q, zero)
    m_s[...] = jnp.full_like(m_s, -1e30)
    l_s[...] = jnp.zeros_like(l_s)
    acc_s[...] = jnp.zeros_like(acc_s)

    def body(j, c):
        k0 = pl.multiple_of(j * tk, tk)
        kc = k_ref[0, pl.ds(k0, tk), :]
        vc = vt_ref[:, pl.ds(k0, tk)]
        s = lax.dot_general(kc, qz_s[...], (((1,), (1,)), ((), ())), preferred_element_type=F32)
        m_old = m_s[...]
        m_new = jnp.maximum(m_old, jnp.max(s, axis=0, keepdims=True))
        alpha = jnp.exp2(m_old - m_new)
        p = jnp.exp2(s - m_new)
        l_s[...] = alpha * l_s[...] + jnp.sum(p, axis=0, keepdims=True)
        pv = jnp.dot(vc, p.astype(BF16), preferred_element_type=F32)
        acc_s[...] = alpha * acc_s[...] + pv
        m_s[...] = m_new
        return c

    lax.fori_loop(0, seq // tk, body, 0)

    lam = (jnp.exp(jnp.sum(lq1_ref[...] * lk1_ref[...], axis=1, keepdims=True))
           - jnp.exp(jnp.sum(lq2_ref[...] * lk2_ref[...], axis=1, keepdims=True)) + LAM_INIT)
    o_all = acc_s[...] / l_s[...]
    o = o_all[:, :tq] - lam * o_all[:, tq:]
    y = o * lax.rsqrt(jnp.mean(o * o, axis=0, keepdims=True) + EPS) * g_ref[...] * (1.0 - LAM_INIT)
    o_ref[0] = y.T.astype(BF16)


def _attn(q3, k3, vt, lam_vecs, subln_g):
    B, S, _ = q3.shape
    tq = _tile(S, 512)
    tk = _tile(S, 512)
    vec = pl.BlockSpec((1, HEAD_DIM), lambda b, h, i: (0, 0))
    kern = functools.partial(_attn_kernel, tq=tq, tk=tk, seq=S)
    return pl.pallas_call(
        kern,
        grid=(B, N_HEADS, S // tq),
        in_specs=[vec, vec, vec, vec,
                  pl.BlockSpec((1, tq, V_DIM), lambda b, h, i: (b, i, h)),
                  pl.BlockSpec((1, S, V_DIM), lambda b, h, i: (b, 0, h)),
                  pl.BlockSpec((V_DIM, S), lambda b, h, i: (h, b)),
                  pl.BlockSpec((V_DIM, 1), lambda b, h, i: (0, 0))],
        out_specs=pl.BlockSpec((1, tq, V_DIM), lambda b, h, i: (b, i, h)),
        out_shape=jax.ShapeDtypeStruct((B, S, N_HEADS * V_DIM), BF16),
        scratch_shapes=[pltpu.VMEM((2 * tq, V_DIM), BF16), pltpu.VMEM((1, 2 * tq), F32),
                        pltpu.VMEM((1, 2 * tq), F32), pltpu.VMEM((V_DIM, 2 * tq), F32)],
        compiler_params=_cparams(("parallel", "parallel", "arbitrary")),
        name="attn",
    )(*lam_vecs, q3, k3, vt, subln_g)


def _gelu_tanh(x):
    return 0.5 * x * (1.0 + jnp.tanh(math.sqrt(2.0 / math.pi) * (x + 0.044715 * (x * x * x))))


def _merge_kernel(x_ref, hf_ref, hb_ref, gr_ref, ga_ref, gb_ref, yb_ref, wout_ref, g_ref,
                  wrh_ref, wrl_ref, br_ref, h1_ref, xn_ref, lt_ref):
    f = lambda ref: ref[...].astype(F32)
    ya = _gelu_tanh(f(gr_ref)) * (f(hf_ref) + f(hb_ref))
    mixed = _sigmoid(f(ga_ref)) * ya + _sigmoid(f(gb_ref)) * f(yb_ref)
    h1 = x_ref[...] + jnp.dot(mixed.astype(BF16), wout_ref[...], preferred_element_type=F32)
    h1_ref[...] = h1
    xn = _rms(h1, g_ref[...])
    xn_hi = xn.astype(BF16)
    xn_ref[...] = xn_hi
    xn_lo = (xn - xn_hi.astype(F32)).astype(BF16)
    nt = lambda w, v: lax.dot_general(w, v, (((1,), (1,)), ((), ())), preferred_element_type=F32)
    lt_ref[...] = nt(wrh_ref[...], xn_hi) + nt(wrh_ref[...], xn_lo) + nt(wrl_ref[...], xn_hi) + br_ref[...]


def _merge(x2d, hf, hb, gr, ga, gb, yb, w_out, g_ffn, wr_hi, wr_lo, b_router):
    T = x2d.shape[0]
    tm = _tile(T, 512)
    tok = pl.BlockSpec((tm, D_MODEL), lambda i: (i, 0))
    const = lambda shape: pl.BlockSpec(shape, lambda i: (0,) * len(shape))
    return pl.pallas_call(
        _merge_kernel,
        grid=(T // tm,),
        in_specs=[tok] * 7 + [const(w_out.shape), const(g_ffn.shape), const(wr_hi.shape),
                              const(wr_lo.shape), const(b_router.shape)],
        out_specs=(tok, tok, pl.BlockSpec((N_EXPERTS, tm), lambda i: (0, i))),
        out_shape=(jax.ShapeDtypeStruct((T, D_MODEL), F32), jax.ShapeDtypeStruct((T, D_MODEL), BF16),
                   jax.ShapeDtypeStruct((N_EXPERTS, T), F32)),
        compiler_params=_cparams(("parallel",)),
        name="merge",
    )(x2d, hf, hb, gr, ga, gb, yb, w_out, g_ffn, wr_hi, wr_lo, b_router)


def _route_kernel(lt_ref, tri_ref, idx_ref, gate_ref, rank_ref, cnt_ref, carry_s, *, tn):
    i = pl.program_id(0)

    @pl.when(i == 0)
    def _():
        carry_s[...] = jnp.zeros_like(carry_s)

    work = lt_ref[...]
    eidx = lax.broadcasted_iota(jnp.int32, work.shape, 0)
    vals, idxs = [], []
    for _ in range(TOP_K):
        mx = jnp.max(work, axis=0, keepdims=True)
        ix = jnp.min(jnp.where(work == mx, eidx, N_EXPERTS), axis=0, keepdims=True)
        vals.append(mx)
        idxs.append(ix)
        work = jnp.where(eidx == ix, -jnp.inf, work)
    ex = [jnp.exp(v - vals[0]) for v in vals]
    denom = ex[0] + ex[1] + ex[2] + ex[3]
    sel = jnp.zeros(work.shape, F32)
    for ix in idxs:
        sel = sel + (eidx == ix).astype(F32)
    before = jnp.dot(sel.astype(BF16), tri_ref[...], preferred_element_type=F32) + carry_s[:, 0:1]
    for k in range(TOP_K):
        rank = jnp.sum(jnp.where(eidx == idxs[k], before, 0.0), axis=0, keepdims=True)
        rank_ref[k:k + 1, :] = rank.astype(jnp.int32)
        idx_ref[k:k + 1, :] = idxs[k]
        gate_ref[k:k + 1, :] = ex[k] / denom
    carry_s[...] = carry_s[...] + jnp.sum(sel, axis=1, keepdims=True)
    cnt_ref[...] = carry_s[...].astype(jnp.int32)


def _route(logits_t):
    E, T = logits_t.shape
    tn = _tile(T, 1024)
    tri = (lax.broadcasted_iota(jnp.int32, (tn, tn), 0) < lax.broadcasted_iota(jnp.int32, (tn, tn), 1)).astype(BF16)
    kt = pl.BlockSpec((TOP_K, tn), lambda i: (0, i))
    kern = functools.partial(_route_kernel, tn=tn)
    return pl.pallas_call(
        kern,
        grid=(T // tn,),
        in_specs=[pl.BlockSpec((E, tn), lambda i: (0, i)), pl.BlockSpec((tn, tn), lambda i: (0, 0))],
        out_specs=(kt, kt, kt, pl.BlockSpec((E, LANES), lambda i: (0, 0))),
        out_shape=(jax.ShapeDtypeStruct((TOP_K, T), jnp.int32), jax.ShapeDtypeStruct((TOP_K, T), F32),
                   jax.ShapeDtypeStruct((TOP_K, T), jnp.int32), jax.ShapeDtypeStruct((E, LANES), jnp.int32)),
        scratch_shapes=[pltpu.VMEM((E, LANES), F32)],
        compiler_params=_cparams(("arbitrary",)),
        name="route",
    )(logits_t, tri)


def _ffn_kernel(be_ref, nu_ref, xs_ref, w1g_ref, w1l_ref, b1g_ref, b1l_ref, w2_ref, b2_ref, ys_ref):
    i = pl.program_id(0)

    @pl.when(i < nu_ref[0])
    def _():
        x = xs_ref[...]
        hg = jnp.dot(x, w1g_ref[0], preferred_element_type=F32) + b1g_ref[0]
        hl = jnp.dot(x, w1l_ref[0], preferred_element_type=F32) + b1l_ref[0]
        xg = jnp.minimum(hg, SWIGLU_LIMIT)
        xl = jnp.clip(hl, -SWIGLU_LIMIT, SWIGLU_LIMIT)
        act = xg * _sigmoid(SWIGLU_ALPHA * xg) * (xl + 1.0)
        ys_ref[...] = jnp.dot(act.astype(BF16), w2_ref[0], preferred_element_type=F32) + b2_ref[0]

    @pl.when(i >= nu_ref[0])
    def _():
        ys_ref[...] = jnp.zeros_like(ys_ref)


def _ffn(block_e, n_used, xs, w1g, w1l, b1g, b1l, w2, b2):
    n_rows = xs.shape[0]
    n_blocks = n_rows // ROWS_PER_BLOCK
    rows = pl.BlockSpec((ROWS_PER_BLOCK, D_MODEL), lambda i, be, nu: (i, 0))
    wspec = pl.BlockSpec((1, D_MODEL, D_MODEL), lambda i, be, nu: (be[i], 0, 0))
    bspec = pl.BlockSpec((1, 1, D_MODEL), lambda i, be, nu: (be[i], 0, 0))
    return pl.pallas_call(
        _ffn_kernel,
        grid_spec=pltpu.PrefetchScalarGridSpec(
            num_scalar_prefetch=2,
            grid=(n_blocks,),
            in_specs=[rows, wspec, wspec, bspec, bspec, wspec, bspec],
            out_specs=rows,
        ),
        out_shape=jax.ShapeDtypeStruct((n_rows, D_MODEL), F32),
        compiler_params=_cparams(("arbitrary",)),
        name="ffn",
    )(block_e, n_used, xs, w1g, w1l, b1g, b1l, w2, b2)


def _final_kernel(h1_ref, y4_ref, gate_ref, p_ref, gpl_ref, wpg_ref, wpe_ref, gfin_ref, out_ref):
    h2 = h1_ref[...]
    for k in range(TOP_K):
        h2 = h2 + gate_ref[:, k:k + 1] * y4_ref[k]
    hn = _rms(h2, gpl_ref[...]).astype(BF16)
    gate = _sigmoid(jnp.dot(hn, wpg_ref[...], preferred_element_type=F32))
    pe = jnp.dot(p_ref[...].astype(BF16), wpe_ref[...], preferred_element_type=F32)
    out_ref[...] = _rms(h2 + gate * pe, gfin_ref[...])


def _final(h1, y4, gates_tk, p2d, g_pl, w_pg, w_pe, g_final):
    T = h1.shape[0]
    tm = _tile(T, 256)
    tok = pl.BlockSpec((tm, D_MODEL), lambda i: (i, 0))
    const = lambda shape: pl.BlockSpec(shape, lambda i: (0,) * len(shape))
    return pl.pallas_call(
        _final_kernel,
        grid=(T // tm,),
        in_specs=[tok, pl.BlockSpec((TOP_K, tm, D_MODEL), lambda i: (0, i, 0)),
                  pl.BlockSpec((tm, TOP_K), lambda i: (i, 0)), pl.BlockSpec((tm, PLE_DIM), lambda i: (i, 0)),
                  const(g_pl.shape), const(w_pg.shape), const(w_pe.shape), const(g_final.shape)],
        out_specs=tok,
        out_shape=jax.ShapeDtypeStruct((T, D_MODEL), F32),
        compiler_params=_cparams(("parallel",)),
        name="final",
    )(h1, y4, gates_tk, p2d, g_pl, w_pg, w_pe, g_final)


def _rope_tables(S):
    pos = jnp.arange(S, dtype=F32)
    inv = ROPE_THETA ** (-jnp.arange(0, HEAD_DIM, 2, dtype=F32) / HEAD_DIM)
    ang = pos[:, None] * inv[None, :]
    ang = jnp.concatenate([ang, ang, ang, ang], axis=-1)
    sign = jnp.where((jnp.arange(LANES) % HEAD_DIM) < HEAD_DIM // 2, -1.0, 1.0).astype(F32)
    return jnp.cos(ang), jnp.sin(ang) * sign


def _block_diag_groups(w):
    per = LRU_GROUP // LRU_BLOCK
    g = w.shape[0] // per
    eye = jnp.eye(per, dtype=w.dtype)
    return jnp.einsum('gbij,bc->gbicj', w.reshape(g, per, LRU_BLOCK, LRU_BLOCK), eye).reshape(g, LRU_GROUP, LRU_GROUP)


def _prep_weights(g_mix, w_in, conv_w, conv_b, lru_wa, lru_ba, lru_wx, lru_bx, lru_lambda,
                  lam_q1, lam_k1, lam_q2, lam_k2, subln_g, w_out, g_ffn, w_router, b_router,
                  w1, b1, w2, b2, g_pl, w_pg, w_pe, g_final):
    l = 0
    W = D_MODEL
    w = w_in[l]
    pw = dict(
        g_mix=g_mix[l][None, :],
        w_rest=jnp.concatenate([w[:, :4 * W], w[:, 5 * W:]], axis=1).astype(BF16),
        w_vt=w[:, 4 * W:5 * W].T.astype(BF16),
        conv_w=conv_w[l], conv_b=conv_b[l][None, :],
        lam_vecs=tuple(v[l][None, :] for v in (lam_q1, lam_k1, lam_q2, lam_k2)),
        subln_g=subln_g[l][:, None],
        w_out=w_out[l].astype(BF16), g_ffn=g_ffn[l][None, :],
        b_router=b_router[l][:, None],
        w1g=w1[l][:, :, 0::2].astype(BF16), w1l=w1[l][:, :, 1::2].astype(BF16),
        b1g=b1[l][:, None, 0::2], b1l=b1[l][:, None, 1::2],
        w2=w2[l].astype(BF16), b2=b2[l][:, None, :],
        g_pl=g_pl[l][None, :], w_pg=w_pg[l].astype(BF16), w_pe=w_pe[l].astype(BF16),
        g_final=g_final[None, :],
    )
    wr_t = w_router[l].T
    wr_hi = wr_t.astype(BF16)
    pw['wr_hi'] = wr_hi
    pw['wr_lo'] = (wr_t - wr_hi.astype(F32)).astype(BF16)
    for d in range(2):
        wa = _block_diag_groups(lru_wa[l, d])
        wx = _block_diag_groups(lru_wx[l, d])
        pw[f'lru_wg{d}'] = jnp.concatenate([wa, wx], axis=2).astype(BF16)
        ng = D_MODEL // LRU_GROUP
        pw[f'lru_bg{d}'] = jnp.concatenate([lru_ba[l, d].reshape(ng, 1, LRU_GROUP),
                                            lru_bx[l, d].reshape(ng, 1, LRU_GROUP)], axis=2)
        pw[f'lru_lam{d}'] = lru_lambda[l, d][None, :]
    return pw


def _moe_plan(idx, rank, counts, T):
    padded = ((counts + ROWS_PER_BLOCK - 1) // ROWS_PER_BLOCK) * ROWS_PER_BLOCK
    pad_end = jnp.cumsum(padded)
    pad_start = pad_end - padded
    dest = pad_start[idx] + rank
    n_blocks = -(-(T * TOP_K) // ROWS_PER_BLOCK) + N_EXPERTS
    block_e = jnp.minimum(
        jnp.searchsorted(pad_end, jnp.arange(n_blocks, dtype=pad_end.dtype) * ROWS_PER_BLOCK, side='right'),
        N_EXPERTS - 1).astype(jnp.int32)
    n_used = (pad_end[-1:] // ROWS_PER_BLOCK).astype(jnp.int32)
    return dest, block_e, n_used, n_blocks * ROWS_PER_BLOCK


def _trunk(x, p, pw):
    B, S, _ = x.shape
    T = B * S
    x2d = x.reshape(T, D_MODEL)
    cos_t, sin_t = _rope_tables(S)
    xr, gr, q, k, vt, ga, gb = _inproj(x2d, pw['g_mix'], cos_t, sin_t, pw['w_rest'], pw['w_vt'], S)
    xr3 = xr.reshape(B, S, D_MODEL)
    hf = _lru(xr3, pw['conv_w'], pw['conv_b'], pw['lru_wg0'], pw['lru_bg0'], pw['lru_lam0'], False)
    hb = _lru(xr3, pw['conv_w'], pw['conv_b'], pw['lru_wg1'], pw['lru_bg1'], pw['lru_lam1'], True)
    yb = _attn(q.reshape(B, S, D_MODEL), k.reshape(B, S, D_MODEL), vt, pw['lam_vecs'], pw['subln_g'])
    h1, xn, logits_t = _merge(x2d, hf.reshape(T, D_MODEL), hb.reshape(T, D_MODEL), gr, ga, gb,
                              yb.reshape(T, D_MODEL), pw['w_out'], pw['g_ffn'],
                              pw['wr_hi'], pw['wr_lo'], pw['b_router'])
    idx, gates, rank, cnt = _route(logits_t)
    dest, block_e, n_used, n_rows = _moe_plan(idx, rank, cnt[:, 0], T)
    tok = jnp.broadcast_to(jnp.arange(T, dtype=jnp.int32)[None, :], (TOP_K, T))
    row_tok = jnp.zeros((n_rows,), jnp.int32).at[dest.reshape(-1)].set(tok.reshape(-1))
    xs = jnp.take(xn, row_tok, axis=0)
    ys = _ffn(block_e, n_used, xs, pw['w1g'], pw['w1l'], pw['b1g'], pw['b1l'], pw['w2'], pw['b2'])
    y4 = jnp.take(ys, dest.reshape(-1), axis=0).reshape(TOP_K, T, D_MODEL)
    out = _final(h1, y4, gates.T, p[0].reshape(T, PLE_DIM), pw['g_pl'], pw['w_pg'], pw['w_pe'], pw['g_final'])
    return out.reshape(B, S, D_MODEL)


def kernel(x_prompt, x_sample, p_prompt, p_sample, g_mix, w_in, conv_w, conv_b, lru_wa, lru_ba, lru_wx, lru_bx, lru_lambda, lam_q1, lam_k1, lam_q2, lam_k2, subln_g, w_out, g_ffn, w_router, b_router, w1, b1, w2, b2, g_pl, w_pg, w_pe, g_final):
    pw = _prep_weights(g_mix, w_in, conv_w, conv_b, lru_wa, lru_ba, lru_wx, lru_bx, lru_lambda,
                       lam_q1, lam_k1, lam_q2, lam_k2, subln_g, w_out, g_ffn, w_router, b_router,
                       w1, b1, w2, b2, g_pl, w_pg, w_pe, g_final)
    return (_trunk(x_prompt, p_prompt, pw), _trunk(x_sample, p_sample, pw))
```

```python
import functools
import math

import jax
import jax.numpy as jnp
from jax import lax
from jax.experimental import pallas as pl
from jax.experimental.pallas import tpu as pltpu

F32 = jnp.float32
BF16 = jnp.bfloat16

D_MODEL = 1024
LRU_BLOCK = 64
LRU_C = 8.0
CONV_W = 4
CONV_LEFT = 2
N_HEADS = 8
HEAD_DIM = 64
V_DIM = 128
ROPE_THETA = 10000.0
N_EXPERTS = 32
TOP_K = 4
SWIGLU_LIMIT = 7.0
SWIGLU_ALPHA = 1.702
ROWS_PER_BLOCK = 512
PLE_DIM = 256
EPS = 1e-6
LAM_INIT = 0.8 - 0.6 * math.exp(-0.3 * 0)
LOG2E = 1.4426950408889634

LANES = 128
SUBLANES = 8
BF16_ROWS = 16
LRU_GROUP = 256
VMEM_LIMIT = 56 * 1024 * 1024


def _tile(n, pref):
    t = min(n, pref)
    while n % t:
        t //= 2
    return t


def _cparams(sem):
    return pltpu.CompilerParams(dimension_semantics=sem, vmem_limit_bytes=VMEM_LIMIT)


def _rms(x, g):
    return x * lax.rsqrt(jnp.mean(x * x, axis=-1, keepdims=True) + EPS) * g


def _sigmoid(x):
    return 1.0 / (1.0 + jnp.exp(-x))


def _inproj_kernel(x_ref, g_ref, cos_ref, sin_ref, w_ref, wvt_ref,
                   xr_ref, gr_ref, q_ref, k_ref, vt_ref, ga_ref, gb_ref):
    u = _rms(x_ref[...], g_ref[...]).astype(BF16)

    def proj(seg):
        return jnp.dot(u, w_ref[:, seg * D_MODEL:(seg + 1) * D_MODEL], preferred_element_type=F32)

    reps = D_MODEL // LANES
    cos = jnp.concatenate([cos_ref[...]] * reps, axis=1)
    sin = jnp.concatenate([sin_ref[...]] * reps, axis=1)
    lane = lax.broadcasted_iota(jnp.int32, cos.shape, 1)
    first_half = (lane % HEAD_DIM) < (HEAD_DIM // 2)

    def rope(z):
        swapped = jnp.where(first_half,
                            pltpu.roll(z, D_MODEL - HEAD_DIM // 2, 1),
                            pltpu.roll(z, HEAD_DIM // 2, 1))
        return z * cos + swapped * sin

    xr_ref[...] = proj(0).astype(BF16)
    gr_ref[...] = proj(1).astype(BF16)
    q_ref[...] = (rope(proj(2)) * (HEAD_DIM ** -0.5 * LOG2E)).astype(BF16)
    k_ref[...] = rope(proj(3)).astype(BF16)
    ga_ref[...] = proj(4).astype(BF16)
    gb_ref[...] = proj(5).astype(BF16)
    vt = lax.dot_general(wvt_ref[...], u, (((1,), (1,)), ((), ())), preferred_element_type=F32)
    vt_ref[...] = vt.astype(BF16)


def _inproj(x2d, g_mix, cos_t, sin_t, w_rest, w_vt, seq):
    T = x2d.shape[0]
    tm = _tile(seq, 512)
    n_pos = seq // tm
    tok = pl.BlockSpec((tm, D_MODEL), lambda i: (i, 0))
    pos = pl.BlockSpec((tm, LANES), lambda i: (i % n_pos, 0))
    const = lambda shape: pl.BlockSpec(shape, lambda i: (0,) * len(shape), pipeline_mode=pl.Buffered(1))
    outs = [jax.ShapeDtypeStruct((T, D_MODEL), BF16)] * 4
    out_shape = (outs[0], outs[1], outs[2], outs[3],
                 jax.ShapeDtypeStruct((D_MODEL, T), BF16), outs[0], outs[0])
    return pl.pallas_call(
        _inproj_kernel,
        grid=(T // tm,),
        in_specs=[tok, const((1, D_MODEL)), pos, pos,
                  const(w_rest.shape), const(w_vt.shape)],
        out_specs=(tok, tok, tok, tok, pl.BlockSpec((D_MODEL, tm), lambda i: (0, i)), tok, tok),
        out_shape=out_shape,
        compiler_params=_cparams(("parallel",)),
        name="inproj",
    )(x2d, g_mix, cos_t, sin_t, w_rest, w_vt)


def _lru_kernel(xm_ref, xp_ref, xn_ref, cw_ref, cb_ref, wg_ref, bg_ref, lam_ref,
                h_ref, a_s, b_s, carry_s, *, reverse, tm, n_tiles):
    i = pl.program_id(1)
    ti = (n_tiles - 1 - i) if reverse else i
    C = D_MODEL

    @pl.when(i == 0)
    def _():
        carry_s[...] = jnp.zeros_like(carry_s)

    xm = xm_ref[0].astype(F32)
    xp = xp_ref[0].astype(F32) * (ti > 0).astype(F32)
    xn = xn_ref[0].astype(F32) * (ti < n_tiles - 1).astype(F32)
    big = jnp.concatenate([xp, xm, xn], axis=0)
    xc = jnp.broadcast_to(cb_ref[...], (tm, C))
    for j in range(CONV_W):
        off = BF16_ROWS + j - CONV_LEFT
        xc = xc + big[off:off + tm, :] * cw_ref[j:j + 1, :]

    xcb = xc.astype(BF16)
    r_parts, i_parts = [], []
    for g in range(C // LRU_GROUP):
        z = jnp.dot(xcb[:, g * LRU_GROUP:(g + 1) * LRU_GROUP], wg_ref[g], preferred_element_type=F32)
        z = z + bg_ref[g]
        r_parts.append(z[:, :LRU_GROUP])
        i_parts.append(z[:, LRU_GROUP:])
    r = _sigmoid(jnp.concatenate(r_parts, axis=1))
    ig = _sigmoid(jnp.concatenate(i_parts, axis=1))

    neg_lam = -lam_ref[...]
    softplus = jnp.maximum(neg_lam, 0.0) + jnp.log1p(jnp.exp(-jnp.abs(neg_lam)))
    a = jnp.exp((-LRU_C * softplus) * r)
    a_s[...] = a
    b_s[...] = jnp.sqrt(1.0 - a * a) * (ig * xc)

    n_slab = tm // SUBLANES
    row = lax.broadcasted_iota(jnp.int32, (SUBLANES, C), 0)

    def body(s, carry):
        sl = (n_slab - 1 - s) if reverse else s
        r0 = pl.multiple_of(sl * SUBLANES, SUBLANES)
        a8 = a_s[pl.ds(r0, SUBLANES), :]
        b8 = b_s[pl.ds(r0, SUBLANES), :]
        for d in (1, 2, 4):
            if reverse:
                valid = row < SUBLANES - d
                shift = SUBLANES - d
            else:
                valid = row >= d
                shift = d
            a_sh = jnp.where(valid, pltpu.roll(a8, shift, 0), 1.0)
            b_sh = jnp.where(valid, pltpu.roll(b8, shift, 0), 0.0)
            b8 = a8 * b_sh + b8
            a8 = a8 * a_sh
        h8 = b8 + a8 * carry
        b_s[pl.ds(r0, SUBLANES), :] = h8
        edge = h8[0:1, :] if reverse else h8[SUBLANES - 1:SUBLANES, :]
        return jnp.broadcast_to(edge, (SUBLANES, C))

    carry_s[...] = lax.fori_loop(0, n_slab, body, carry_s[...])
    h_ref[0] = b_s[...].astype(BF16)


def _lru(xr3, conv_w, conv_b, wg, bg, lam, reverse):
    B, S, C = xr3.shape
    tm = _tile(S, 256)
    n_tiles = S // tm
    hb = tm // BF16_ROWS
    n_halo = S // BF16_ROWS
    tidx = (lambda i: n_tiles - 1 - i) if reverse else (lambda i: i)
    const = lambda shape: pl.BlockSpec(shape, lambda b, i: (0,) * len(shape))
    kern = functools.partial(_lru_kernel, reverse=reverse, tm=tm, n_tiles=n_tiles)
    return pl.pallas_call(
        kern,
        grid=(B, n_tiles),
        in_specs=[
            pl.BlockSpec((1, tm, C), lambda b, i: (b, tidx(i), 0)),
            pl.BlockSpec((1, BF16_ROWS, C), lambda b, i: (b, jnp.maximum(tidx(i) * hb - 1, 0), 0)),
            pl.BlockSpec((1, BF16_ROWS, C), lambda b, i: (b, jnp.minimum((tidx(i) + 1) * hb, n_halo - 1), 0)),
            const(conv_w.shape), const(conv_b.shape), const(wg.shape), const(bg.shape), const(lam.shape),
        ],
        out_specs=pl.BlockSpec((1, tm, C), lambda b, i: (b, tidx(i), 0)),
        out_shape=jax.ShapeDtypeStruct((B, S, C), BF16),
        scratch_shapes=[pltpu.VMEM((tm, C), F32), pltpu.VMEM((tm, C), F32), pltpu.VMEM((SUBLANES, C), F32)],
        compiler_params=_cparams(("parallel", "arbitrary")),
        name="lru_bwd" if reverse else "lru_fwd",
    )(xr3, xr3, xr3, conv_w, conv_b, wg, bg, lam)


def _attn_kernel(lq1_ref, lk1_ref, lq2_ref, lk2_ref, q_ref, k_ref, vt_ref, g_ref,
                 o_ref, qz_s, s_s, m_s, l_s, acc_s, *, tq, tk, seq):
    q = q_ref[0]
    lane = lax.broadcasted_iota(jnp.int32, q.shape, 1)
    zero = jnp.zeros_like(q)
    qz_s[0:tq, :] = jnp.where(lane < HEAD_DIM, q, zero)
    qz_s[tq:2 * tq, :] = jnp.where(lane >= HEAD_DIM, q, zero)
    m_s[...] = jnp.full_like(m_s, -1e30)
    l_s[...] = jnp.zeros_like(l_s)
    acc_s[...] = jnp.zeros_like(acc_s)
    n_chunks = seq // tk

    def scores(j, slot):
        k0 = pl.multiple_of(j * tk, tk)
        kc = k_ref[0, pl.ds(k0, tk), :]
        s_s[slot] = lax.dot_general(kc, qz_s[...], (((1,), (1,)), ((), ())),
                                    preferred_element_type=F32)

    def accumulate(j, slot):
        k0 = pl.multiple_of(j * tk, tk)
        vc = vt_ref[:, pl.ds(k0, tk)]
        s = s_s[slot]
        m_old = m_s[...]
        m_new = jnp.maximum(m_old, jnp.max(s, axis=0, keepdims=True))
        alpha = jnp.exp2(m_old - m_new)
        p = jnp.exp2(s - m_new)
        l_s[...] = alpha * l_s[...] + jnp.sum(p, axis=0, keepdims=True)
        pv = jnp.dot(vc, p.astype(BF16), preferred_element_type=F32)
        acc_s[...] = alpha * acc_s[...] + pv
        m_s[...] = m_new

    scores(0, 0)

    def body(jj, c):
        j = 2 * jj
        scores(j + 1, 1)
        accumulate(j, 0)
        scores(jnp.minimum(j + 2, n_chunks - 1), 0)
        accumulate(j + 1, 1)
        return c

    lax.fori_loop(0, n_chunks // 2, body, 0)

    lam = (jnp.exp(jnp.sum(lq1_ref[...] * lk1_ref[...], axis=1, keepdims=True))
           - jnp.exp(jnp.sum(lq2_ref[...] * lk2_ref[...], axis=1, keepdims=True)) + LAM_INIT)
    o_all = acc_s[...] / l_s[...]
    o = o_all[:, :tq] - lam * o_all[:, tq:]
    y = o * lax.rsqrt(jnp.mean(o * o, axis=0, keepdims=True) + EPS) * g_ref[...] * (1.0 - LAM_INIT)
    o_ref[0] = y.T.astype(BF16)


def _attn(q3, k3, vt, lam_vecs, subln_g):
    B, S, _ = q3.shape
    tq = _tile(S, 512)
    tk = _tile(S, 512)
    assert (S // tk) % 2 == 0, "the key loop is unrolled by two"
    vec = pl.BlockSpec((1, HEAD_DIM), lambda b, h, i: (0, 0))
    kern = functools.partial(_attn_kernel, tq=tq, tk=tk, seq=S)
    return pl.pallas_call(
        kern,
        grid=(B, N_HEADS, S // tq),
        in_specs=[vec, vec, vec, vec,
                  pl.BlockSpec((1, tq, V_DIM), lambda b, h, i: (b, i, h)),
                  pl.BlockSpec((1, S, V_DIM), lambda b, h, i: (b, 0, h)),
                  pl.BlockSpec((V_DIM, S), lambda b, h, i: (h, b)),
                  pl.BlockSpec((V_DIM, 1), lambda b, h, i: (0, 0))],
        out_specs=pl.BlockSpec((1, tq, V_DIM), lambda b, h, i: (b, i, h)),
        out_shape=jax.ShapeDtypeStruct((B, S, N_HEADS * V_DIM), BF16),
        scratch_shapes=[pltpu.VMEM((2 * tq, V_DIM), BF16), pltpu.VMEM((2, tk, 2 * tq), F32),
                        pltpu.VMEM((1, 2 * tq), F32), pltpu.VMEM((1, 2 * tq), F32),
                        pltpu.VMEM((V_DIM, 2 * tq), F32)],
        compiler_params=_cparams(("parallel", "parallel", "arbitrary")),
        name="attn",
    )(*lam_vecs, q3, k3, vt, subln_g)


def _gelu_tanh(x):
    return 0.5 * x * (1.0 + jnp.tanh(math.sqrt(2.0 / math.pi) * (x + 0.044715 * (x * x * x))))


def _merge_kernel(x_ref, hf_ref, hb_ref, gr_ref, ga_ref, gb_ref, yb_ref, wout_ref, g_ref,
                  wrh_ref, wrl_ref, br_ref, h1_ref, xn_ref, lt_ref):
    f = lambda ref: ref[...].astype(F32)
    ya = _gelu_tanh(f(gr_ref)) * (f(hf_ref) + f(hb_ref))
    mixed = _sigmoid(f(ga_ref)) * ya + _sigmoid(f(gb_ref)) * f(yb_ref)
    h1 = x_ref[...] + jnp.dot(mixed.astype(BF16), wout_ref[...], preferred_element_type=F32)
    h1_ref[...] = h1
    xn = _rms(h1, g_ref[...])
    xn_hi = xn.astype(BF16)
    xn_ref[...] = xn_hi
    xn_lo = (xn - xn_hi.astype(F32)).astype(BF16)
    nt = lambda w, v: lax.dot_general(w, v, (((1,), (1,)), ((), ())), preferred_element_type=F32)
    lt_ref[...] = nt(wrh_ref[...], xn_hi) + nt(wrh_ref[...], xn_lo) + nt(wrl_ref[...], xn_hi) + br_ref[...]


def _merge(x2d, hf, hb, gr, ga, gb, yb, w_out, g_ffn, wr_hi, wr_lo, b_router):
    T = x2d.shape[0]
    tm = _tile(T, 512)
    tok = pl.BlockSpec((tm, D_MODEL), lambda i: (i, 0))
    const = lambda shape: pl.BlockSpec(shape, lambda i: (0,) * len(shape))
    return pl.pallas_call(
        _merge_kernel,
        grid=(T // tm,),
        in_specs=[tok] * 7 + [const(w_out.shape), const(g_ffn.shape), const(wr_hi.shape),
                              const(wr_lo.shape), const(b_router.shape)],
        out_specs=(tok, tok, pl.BlockSpec((N_EXPERTS, tm), lambda i: (0, i))),
        out_shape=(jax.ShapeDtypeStruct((T, D_MODEL), F32), jax.ShapeDtypeStruct((T, D_MODEL), BF16),
                   jax.ShapeDtypeStruct((N_EXPERTS, T), F32)),
        compiler_params=_cparams(("parallel",)),
        name="merge",
    )(x2d, hf, hb, gr, ga, gb, yb, w_out, g_ffn, wr_hi, wr_lo, b_router)


def _route_kernel(lt_ref, tri_ref, idx_ref, gate_ref, rank_ref, cnt_ref, carry_s, *, tn):
    i = pl.program_id(0)

    @pl.when(i == 0)
    def _():
        carry_s[...] = jnp.zeros_like(carry_s)

    work = lt_ref[...]
    eidx = lax.broadcasted_iota(jnp.int32, work.shape, 0)
    vals, idxs = [], []
    for _ in range(TOP_K):
        mx = jnp.max(work, axis=0, keepdims=True)
        ix = jnp.min(jnp.where(work == mx, eidx, N_EXPERTS), axis=0, keepdims=True)
        vals.append(mx)
        idxs.append(ix)
        work = jnp.where(eidx == ix, -jnp.inf, work)
    ex = [jnp.exp(v - vals[0]) for v in vals]
    denom = ex[0] + ex[1] + ex[2] + ex[3]
    sel = jnp.zeros(work.shape, F32)
    for ix in idxs:
        sel = sel + (eidx == ix).astype(F32)
    before = jnp.dot(sel.astype(BF16), tri_ref[...], preferred_element_type=F32) + carry_s[:, 0:1]
    for k in range(TOP_K):
        rank = jnp.sum(jnp.where(eidx == idxs[k], before, 0.0), axis=0, keepdims=True)
        rank_ref[k:k + 1, :] = rank.astype(jnp.int32)
        idx_ref[k:k + 1, :] = idxs[k]
        gate_ref[k:k + 1, :] = ex[k] / denom
    carry_s[...] = carry_s[...] + jnp.sum(sel, axis=1, keepdims=True)
    cnt_ref[...] = carry_s[...].astype(jnp.int32)


def _route(logits_t):
    E, T = logits_t.shape
    tn = _tile(T, 1024)
    tri = (lax.broadcasted_iota(jnp.int32, (tn, tn), 0) < lax.broadcasted_iota(jnp.int32, (tn, tn), 1)).astype(BF16)
    kt = pl.BlockSpec((TOP_K, tn), lambda i: (0, i))
    kern = functools.partial(_route_kernel, tn=tn)
    return pl.pallas_call(
        kern,
        grid=(T // tn,),
        in_specs=[pl.BlockSpec((E, tn), lambda i: (0, i)), pl.BlockSpec((tn, tn), lambda i: (0, 0))],
        out_specs=(kt, kt, kt, pl.BlockSpec((E, LANES), lambda i: (0, 0))),
        out_shape=(jax.ShapeDtypeStruct((TOP_K, T), jnp.int32), jax.ShapeDtypeStruct((TOP_K, T), F32),
                   jax.ShapeDtypeStruct((TOP_K, T), jnp.int32), jax.ShapeDtypeStruct((E, LANES), jnp.int32)),
        scratch_shapes=[pltpu.VMEM((E, LANES), F32)],
        compiler_params=_cparams(("arbitrary",)),
        name="route",
    )(logits_t, tri)


def _split_w1_kernel(w_ref, perm_ref, wg_ref, wl_ref):
    z = jnp.dot(w_ref[0].astype(BF16), perm_ref[...], preferred_element_type=F32)
    wg_ref[0] = z[:, :D_MODEL].astype(BF16)
    wl_ref[0] = z[:, D_MODEL:].astype(BF16)


def _split_w1(w1):
    E, D, D2 = w1.shape
    tr = _tile(D, 512)
    col = lax.broadcasted_iota(jnp.int32, (D2, D2), 1)
    row = lax.broadcasted_iota(jnp.int32, (D2, D2), 0)
    perm = (row == jnp.where(col < D2 // 2, 2 * col, 2 * (col - D2 // 2) + 1)).astype(BF16)
    out = pl.BlockSpec((1, tr, D2 // 2), lambda e, i: (e, i, 0))
    return pl.pallas_call(
        _split_w1_kernel,
        grid=(E, D // tr),
        in_specs=[pl.BlockSpec((1, tr, D2), lambda e, i: (e, i, 0)),
                  pl.BlockSpec((D2, D2), lambda e, i: (0, 0), pipeline_mode=pl.Buffered(1))],
        out_specs=(out, out),
        out_shape=(jax.ShapeDtypeStruct((E, D, D2 // 2), BF16),) * 2,
        compiler_params=_cparams(("parallel", "parallel")),
        name="split_w1",
    )(w1, perm)


def _ffn_kernel(be_ref, nu_ref, xs_ref, w1g_ref, w1l_ref, b1g_ref, b1l_ref, w2_ref, b2_ref, ys_ref):
    i = pl.program_id(0)

    @pl.when(i < nu_ref[0])
    def _():
        x = xs_ref[...]
        hg = jnp.dot(x, w1g_ref[0], preferred_element_type=F32) + b1g_ref[0]
        hl = jnp.dot(x, w1l_ref[0], preferred_element_type=F32) + b1l_ref[0]
        xg = jnp.minimum(hg, SWIGLU_LIMIT)
        xl = jnp.clip(hl, -SWIGLU_LIMIT, SWIGLU_LIMIT)
        act = xg * _sigmoid(SWIGLU_ALPHA * xg) * (xl + 1.0)
        ys_ref[...] = jnp.dot(act.astype(BF16), w2_ref[0], preferred_element_type=F32) + b2_ref[0]

    @pl.when(i >= nu_ref[0])
    def _():
        ys_ref[...] = jnp.zeros_like(ys_ref)


def _ffn(block_e, n_used, xs, w1g, w1l, b1g, b1l, w2, b2):
    n_rows = xs.shape[0]
    n_blocks = n_rows // ROWS_PER_BLOCK
    rows = pl.BlockSpec((ROWS_PER_BLOCK, D_MODEL), lambda i, be, nu: (i, 0))
    wspec = pl.BlockSpec((1, D_MODEL, D_MODEL), lambda i, be, nu: (be[i], 0, 0))
    bspec = pl.BlockSpec((1, 1, D_MODEL), lambda i, be, nu: (be[i], 0, 0))
    return pl.pallas_call(
        _ffn_kernel,
        grid_spec=pltpu.PrefetchScalarGridSpec(
            num_scalar_prefetch=2,
            grid=(n_blocks,),
            in_specs=[rows, wspec, wspec, bspec, bspec, wspec, bspec],
            out_specs=rows,
        ),
        out_shape=jax.ShapeDtypeStruct((n_rows, D_MODEL), F32),
        compiler_params=_cparams(("arbitrary",)),
        name="ffn",
    )(block_e, n_used, xs, w1g, w1l, b1g, b1l, w2, b2)


def _final_kernel(h1_ref, y4_ref, gate_ref, p_ref, gpl_ref, wpg_ref, wpe_ref, gfin_ref, out_ref):
    h2 = h1_ref[...]
    for k in range(TOP_K):
        h2 = h2 + gate_ref[:, k:k + 1] * y4_ref[k]
    hn = _rms(h2, gpl_ref[...]).astype(BF16)
    gate = _sigmoid(jnp.dot(hn, wpg_ref[...], preferred_element_type=F32))
    pe = jnp.dot(p_ref[...].astype(BF16), wpe_ref[...], preferred_element_type=F32)
    out_ref[...] = _rms(h2 + gate * pe, gfin_ref[...])


def _final(h1, y4, gates_tk, p2d, g_pl, w_pg, w_pe, g_final):
    T = h1.shape[0]
    tm = _tile(T, 256)
    tok = pl.BlockSpec((tm, D_MODEL), lambda i: (i, 0))
    const = lambda shape: pl.BlockSpec(shape, lambda i: (0,) * len(shape))
    return pl.pallas_call(
        _final_kernel,
        grid=(T // tm,),
        in_specs=[tok, pl.BlockSpec((TOP_K, tm, D_MODEL), lambda i: (0, i, 0)),
                  pl.BlockSpec((tm, TOP_K), lambda i: (i, 0)), pl.BlockSpec((tm, PLE_DIM), lambda i: (i, 0)),
                  const(g_pl.shape), const(w_pg.shape), const(w_pe.shape), const(g_final.shape)],
        out_specs=tok,
        out_shape=jax.ShapeDtypeStruct((T, D_MODEL), F32),
        compiler_params=_cparams(("parallel",)),
        name="final",
    )(h1, y4, gates_tk, p2d, g_pl, w_pg, w_pe, g_final)


def _rope_tables(S):
    pos = jnp.arange(S, dtype=F32)
    inv = ROPE_THETA ** (-jnp.arange(0, HEAD_DIM, 2, dtype=F32) / HEAD_DIM)
    ang = pos[:, None] * inv[None, :]
    ang = jnp.concatenate([ang, ang, ang, ang], axis=-1)
    sign = jnp.where((jnp.arange(LANES) % HEAD_DIM) < HEAD_DIM // 2, -1.0, 1.0).astype(F32)
    return jnp.cos(ang), jnp.sin(ang) * sign


def _block_diag_groups(w):
    per = LRU_GROUP // LRU_BLOCK
    g = w.shape[0] // per
    eye = jnp.eye(per, dtype=w.dtype)
    return jnp.einsum('gbij,bc->gbicj', w.reshape(g, per, LRU_BLOCK, LRU_BLOCK), eye).reshape(g, LRU_GROUP, LRU_GROUP)


def _prep_weights(g_mix, w_in, conv_w, conv_b, lru_wa, lru_ba, lru_wx, lru_bx, lru_lambda,
                  lam_q1, lam_k1, lam_q2, lam_k2, subln_g, w_out, g_ffn, w_router, b_router,
                  w1, b1, w2, b2, g_pl, w_pg, w_pe, g_final):
    l = 0
    W = D_MODEL
    w = w_in[l]
    pw = dict(
        g_mix=g_mix[l][None, :],
        w_rest=jnp.concatenate([w[:, :4 * W], w[:, 5 * W:]], axis=1).astype(BF16),
        w_vt=w[:, 4 * W:5 * W].T.astype(BF16),
        conv_w=conv_w[l], conv_b=conv_b[l][None, :],
        lam_vecs=tuple(v[l][None, :] for v in (lam_q1, lam_k1, lam_q2, lam_k2)),
        subln_g=subln_g[l][:, None],
        w_out=w_out[l].astype(BF16), g_ffn=g_ffn[l][None, :],
        b_router=b_router[l][:, None],
        b1g=b1[l][:, None, 0::2], b1l=b1[l][:, None, 1::2],
        w2=w2[l].astype(BF16), b2=b2[l][:, None, :],
        g_pl=g_pl[l][None, :], w_pg=w_pg[l].astype(BF16), w_pe=w_pe[l].astype(BF16),
        g_final=g_final[None, :],
    )
    pw['w1g'], pw['w1l'] = _split_w1(w1[l])
    wr_t = w_router[l].T
    wr_hi = wr_t.astype(BF16)
    pw['wr_hi'] = wr_hi
    pw['wr_lo'] = (wr_t - wr_hi.astype(F32)).astype(BF16)
    for d in range(2):
        wa = _block_diag_groups(lru_wa[l, d])
        wx = _block_diag_groups(lru_wx[l, d])
        pw[f'lru_wg{d}'] = jnp.concatenate([wa, wx], axis=2).astype(BF16)
        ng = D_MODEL // LRU_GROUP
        pw[f'lru_bg{d}'] = jnp.concatenate([lru_ba[l, d].reshape(ng, 1, LRU_GROUP),
                                            lru_bx[l, d].reshape(ng, 1, LRU_GROUP)], axis=2)
        pw[f'lru_lam{d}'] = lru_lambda[l, d][None, :]
    return pw


def _moe_plan(idx, rank, counts, T):
    padded = ((counts + ROWS_PER_BLOCK - 1) // ROWS_PER_BLOCK) * ROWS_PER_BLOCK
    pad_end = jnp.cumsum(padded)
    pad_start = pad_end - padded
    dest = pad_start[idx] + rank
    n_blocks = -(-(T * TOP_K) // ROWS_PER_BLOCK) + N_EXPERTS
    block_start = jnp.arange(n_blocks, dtype=pad_end.dtype) * ROWS_PER_BLOCK
    block_e = jnp.minimum(jnp.sum(pad_end[None, :] <= block_start[:, None], axis=1),
                          N_EXPERTS - 1).astype(jnp.int32)
    n_used = (pad_end[-1:] // ROWS_PER_BLOCK).astype(jnp.int32)
    return dest, block_e, n_used, n_blocks * ROWS_PER_BLOCK


def _trunk(x, p, pw):
    B, S, _ = x.shape
    T = B * S
    x2d = x.reshape(T, D_MODEL)
    cos_t, sin_t = _rope_tables(S)
    xr, gr, q, k, vt, ga, gb = _inproj(x2d, pw['g_mix'], cos_t, sin_t, pw['w_rest'], pw['w_vt'], S)
    xr3 = xr.reshape(B, S, D_MODEL)
    hf = _lru(xr3, pw['conv_w'], pw['conv_b'], pw['lru_wg0'], pw['lru_bg0'], pw['lru_lam0'], False)
    hb = _lru(xr3, pw['conv_w'], pw['conv_b'], pw['lru_wg1'], pw['lru_bg1'], pw['lru_lam1'], True)
    yb = _attn(q.reshape(B, S, D_MODEL), k.reshape(B, S, D_MODEL), vt, pw['lam_vecs'], pw['subln_g'])
    h1, xn, logits_t = _merge(x2d, hf.reshape(T, D_MODEL), hb.reshape(T, D_MODEL), gr, ga, gb,
                              yb.reshape(T, D_MODEL), pw['w_out'], pw['g_ffn'],
                              pw['wr_hi'], pw['wr_lo'], pw['b_router'])
    idx, gates, rank, cnt = _route(logits_t)
    dest, block_e, n_used, n_rows = _moe_plan(idx, rank, cnt[:, 0], T)
    tok = jnp.broadcast_to(jnp.arange(T, dtype=jnp.int32)[None, :], (TOP_K, T))
    row_tok = jnp.zeros((n_rows,), jnp.int32).at[dest.reshape(-1)].set(tok.reshape(-1))
    xs = jnp.take(xn, row_tok, axis=0)
    ys = _ffn(block_e, n_used, xs, pw['w1g'], pw['w1l'], pw['b1g'], pw['b1l'], pw['w2'], pw['b2'])
    y4 = jnp.take(ys, dest.reshape(-1), axis=0).reshape(TOP_K, T, D_MODEL)
    out = _final(h1, y4, gates.T, p[0].reshape(T, PLE_DIM), pw['g_pl'], pw['w_pg'], pw['w_pe'], pw['g_final'])
    return out.reshape(B, S, D_MODEL)


def kernel(x_prompt, x_sample, p_prompt, p_sample, g_mix, w_in, conv_w, conv_b, lru_wa, lru_ba, lru_wx, lru_bx, lru_lambda, lam_q1, lam_k1, lam_q2, lam_k2, subln_g, w_out, g_ffn, w_router, b_router, w1, b1, w2, b2, g_pl, w_pg, w_pe, g_final):
    pw = _prep_weights(g_mix, w_in, conv_w, conv_b, lru_wa, lru_ba, lru_wx, lru_bx, lru_lambda,
                       lam_q1, lam_k1, lam_q2, lam_k2, subln_g, w_out, g_ffn, w_router, b_router,
                       w1, b1, w2, b2, g_pl, w_pg, w_pe, g_final)
    return (_trunk(x_prompt, p_prompt, pw), _trunk(x_sample, p_sample, pw))
```

```python
import functools
import math

import jax
import jax.numpy as jnp
from jax import lax
from jax.experimental import pallas as pl
from jax.experimental.pallas import tpu as pltpu

F32 = jnp.float32
BF16 = jnp.bfloat16

D_MODEL = 1024
LRU_BLOCK = 64
LRU_C = 8.0
CONV_W = 4
CONV_LEFT = 2
N_HEADS = 8
HEAD_DIM = 64
V_DIM = 128
ROPE_THETA = 10000.0
N_EXPERTS = 32
TOP_K = 4
SWIGLU_LIMIT = 7.0
SWIGLU_ALPHA = 1.702
ROWS_PER_BLOCK = 512
PLE_DIM = 256
EPS = 1e-6
LAM_INIT = 0.8 - 0.6 * math.exp(-0.3 * 0)
LOG2E = 1.4426950408889634

LANES = 128
SUBLANES = 8
BF16_ROWS = 16
MXU_N = 256
LRU_GROUP = MXU_N
VMEM_LIMIT = 56 * 1024 * 1024


def _tile(n, pref):
    t = min(n, pref)
    while n % t:
        t //= 2
    return t


def _cparams(sem):
    return pltpu.CompilerParams(dimension_semantics=sem, vmem_limit_bytes=VMEM_LIMIT)


def _rms(x, g):
    return x * lax.rsqrt(jnp.mean(x * x, axis=-1, keepdims=True) + EPS) * g


def _sigmoid(x):
    return 1.0 / (1.0 + jnp.exp(-x))


def _inproj_kernel(x_ref, g_ref, cos_ref, sin_ref, w_ref, wvt_ref,
                   xr_ref, gr_ref, q_ref, k_ref, vt_ref, ga_ref, gb_ref):
    u = _rms(x_ref[...], g_ref[...]).astype(BF16)

    def proj(seg):
        return jnp.dot(u, w_ref[:, seg * D_MODEL:(seg + 1) * D_MODEL], preferred_element_type=F32)

    reps = D_MODEL // LANES
    cos = jnp.concatenate([cos_ref[...]] * reps, axis=1)
    sin = jnp.concatenate([sin_ref[...]] * reps, axis=1)
    lane = lax.broadcasted_iota(jnp.int32, cos.shape, 1)
    first_half = (lane % HEAD_DIM) < (HEAD_DIM // 2)

    def rope(z):
        swapped = jnp.where(first_half,
                            pltpu.roll(z, D_MODEL - HEAD_DIM // 2, 1),
                            pltpu.roll(z, HEAD_DIM // 2, 1))
        return z * cos + swapped * sin

    xr_ref[...] = proj(0).astype(BF16)
    gr_ref[...] = proj(1).astype(BF16)
    q_ref[...] = (rope(proj(2)) * (HEAD_DIM ** -0.5 * LOG2E)).astype(BF16)
    k_ref[...] = rope(proj(3)).astype(BF16)
    ga_ref[...] = proj(4).astype(BF16)
    gb_ref[...] = proj(5).astype(BF16)
    vt = lax.dot_general(wvt_ref[...], u, (((1,), (1,)), ((), ())), preferred_element_type=F32)
    vt_ref[...] = vt.astype(BF16)


def _inproj(x2d, g_mix, cos_t, sin_t, w_rest, w_vt, seq):
    T = x2d.shape[0]
    tm = _tile(seq, 512)
    n_pos = seq // tm
    tok = pl.BlockSpec((tm, D_MODEL), lambda i: (i, 0))
    pos = pl.BlockSpec((tm, LANES), lambda i: (i % n_pos, 0))
    const = lambda shape: pl.BlockSpec(shape, lambda i: (0,) * len(shape), pipeline_mode=pl.Buffered(1))
    outs = [jax.ShapeDtypeStruct((T, D_MODEL), BF16)] * 4
    out_shape = (outs[0], outs[1], outs[2], outs[3],
                 jax.ShapeDtypeStruct((D_MODEL, T), BF16), outs[0], outs[0])
    return pl.pallas_call(
        _inproj_kernel,
        grid=(T // tm,),
        in_specs=[tok, const((1, D_MODEL)), pos, pos,
                  const(w_rest.shape), const(w_vt.shape)],
        out_specs=(tok, tok, tok, tok, pl.BlockSpec((D_MODEL, tm), lambda i: (0, i)), tok, tok),
        out_shape=out_shape,
        compiler_params=_cparams(("parallel",)),
        name="inproj",
    )(x2d, g_mix, cos_t, sin_t, w_rest, w_vt)


def _lru_kernel(xm_ref, xp_ref, xn_ref, cw_ref, cb_ref, wg_ref, bg_ref, lam_ref,
                h_ref, a_s, b_s, carry_s, *, reverse, tm, n_tiles):
    i = pl.program_id(1)
    ti = (n_tiles - 1 - i) if reverse else i
    C = D_MODEL

    @pl.when(i == 0)
    def _():
        carry_s[...] = jnp.zeros_like(carry_s)

    xm = xm_ref[0].astype(F32)
    xp = xp_ref[0].astype(F32) * (ti > 0).astype(F32)
    xn = xn_ref[0].astype(F32) * (ti < n_tiles - 1).astype(F32)
    big = jnp.concatenate([xp, xm, xn], axis=0)
    xc = jnp.broadcast_to(cb_ref[...], (tm, C))
    for j in range(CONV_W):
        off = BF16_ROWS + j - CONV_LEFT
        xc = xc + big[off:off + tm, :] * cw_ref[j:j + 1, :]

    xcb = xc.astype(BF16)
    r_parts, i_parts = [], []
    for g in range(C // LRU_GROUP):
        z = jnp.dot(xcb[:, g * LRU_GROUP:(g + 1) * LRU_GROUP], wg_ref[g], preferred_element_type=F32)
        z = z + bg_ref[g]
        r_parts.append(z[:, :LRU_GROUP])
        i_parts.append(z[:, LRU_GROUP:])
    r = _sigmoid(jnp.concatenate(r_parts, axis=1))
    ig = _sigmoid(jnp.concatenate(i_parts, axis=1))

    neg_lam = -lam_ref[...]
    softplus = jnp.maximum(neg_lam, 0.0) + jnp.log1p(jnp.exp(-jnp.abs(neg_lam)))
    a = jnp.exp((-LRU_C * softplus) * r)
    y = 1.0 - a * a
    a_s[...] = a
    b_s[...] = (y * lax.rsqrt(jnp.maximum(y, 1e-30))) * (ig * xc)

    n_slab = tm // SUBLANES
    row = lax.broadcasted_iota(jnp.int32, (SUBLANES, C), 0)

    def body(s, carry):
        sl = (n_slab - 1 - s) if reverse else s
        r0 = pl.multiple_of(sl * SUBLANES, SUBLANES)
        a8 = a_s[pl.ds(r0, SUBLANES), :]
        b8 = b_s[pl.ds(r0, SUBLANES), :]
        for d in (1, 2, 4):
            if reverse:
                valid = row < SUBLANES - d
                shift = SUBLANES - d
            else:
                valid = row >= d
                shift = d
            a_sh = jnp.where(valid, pltpu.roll(a8, shift, 0), 1.0)
            b_sh = jnp.where(valid, pltpu.roll(b8, shift, 0), 0.0)
            b8 = a8 * b_sh + b8
            a8 = a8 * a_sh
        h8 = b8 + a8 * carry
        b_s[pl.ds(r0, SUBLANES), :] = h8
        edge = h8[0:1, :] if reverse else h8[SUBLANES - 1:SUBLANES, :]
        return jnp.broadcast_to(edge, (SUBLANES, C))

    carry_s[...] = lax.fori_loop(0, n_slab, body, carry_s[...])
    h_ref[0] = b_s[...].astype(BF16)


def _lru(xr3, conv_w, conv_b, wg, bg, lam, reverse):
    B, S, C = xr3.shape
    tm = _tile(S, 256)
    n_tiles = S // tm
    hb = tm // BF16_ROWS
    n_halo = S // BF16_ROWS
    tidx = (lambda i: n_tiles - 1 - i) if reverse else (lambda i: i)
    const = lambda shape: pl.BlockSpec(shape, lambda b, i: (0,) * len(shape))
    kern = functools.partial(_lru_kernel, reverse=reverse, tm=tm, n_tiles=n_tiles)
    return pl.pallas_call(
        kern,
        grid=(B, n_tiles),
        in_specs=[
            pl.BlockSpec((1, tm, C), lambda b, i: (b, tidx(i), 0)),
            pl.BlockSpec((1, BF16_ROWS, C), lambda b, i: (b, jnp.maximum(tidx(i) * hb - 1, 0), 0)),
            pl.BlockSpec((1, BF16_ROWS, C), lambda b, i: (b, jnp.minimum((tidx(i) + 1) * hb, n_halo - 1), 0)),
            const(conv_w.shape), const(conv_b.shape), const(wg.shape), const(bg.shape), const(lam.shape),
        ],
        out_specs=pl.BlockSpec((1, tm, C), lambda b, i: (b, tidx(i), 0)),
        out_shape=jax.ShapeDtypeStruct((B, S, C), BF16),
        scratch_shapes=[pltpu.VMEM((tm, C), F32), pltpu.VMEM((tm, C), F32), pltpu.VMEM((SUBLANES, C), F32)],
        compiler_params=_cparams(("parallel", "arbitrary")),
        name="lru_bwd" if reverse else "lru_fwd",
    )(xr3, xr3, xr3, conv_w, conv_b, wg, bg, lam)


def _attn_kernel(lq1_ref, lk1_ref, lq2_ref, lk2_ref, q_ref, k_ref, vt_ref, g_ref,
                 o_ref, qz_s, s_s, m_s, l_s, acc_s, *, tq, tk, seq, unroll):
    qt = q_ref[0].astype(F32).T
    chan = lax.broadcasted_iota(jnp.int32, qt.shape, 0)
    qz_s[:, 0:tq] = jnp.where(chan < HEAD_DIM, qt, 0.0).astype(BF16)
    qz_s[:, tq:2 * tq] = jnp.where(chan >= HEAD_DIM, qt, 0.0).astype(BF16)
    m_s[...] = jnp.full_like(m_s, -1e30)
    l_s[...] = jnp.zeros_like(l_s)
    acc_s[...] = jnp.zeros_like(acc_s)
    n_chunks = seq // tk

    n_ct = 2 * tq // MXU_N

    def scores(kc, slot, ct):
        cols = slice(ct * MXU_N, (ct + 1) * MXU_N)
        s_s[slot, ct] = jnp.dot(kc, qz_s[:, cols], preferred_element_type=F32)

    def accumulate(va, slot, ct):
        cols = slice(ct * MXU_N, (ct + 1) * MXU_N)
        s = s_s[slot, ct]
        m_old = m_s[:, cols]
        m_new = jnp.maximum(m_old, jnp.max(s, axis=0, keepdims=True))
        alpha = jnp.exp2(m_old - m_new)
        p = jnp.exp2(s - m_new)
        pv = jnp.dot(va, p.astype(BF16), preferred_element_type=F32)
        l_s[:, cols] = alpha * l_s[:, cols] + pv[V_DIM:V_DIM + 1, :]
        acc_s[:, cols] = alpha * acc_s[:, cols] + pv[:V_DIM, :]
        m_s[:, cols] = m_new

    def keys(j):
        return k_ref[0, pl.ds(pl.multiple_of(j * tk, tk), tk), :]

    def values(j):
        vc = vt_ref[:, pl.ds(pl.multiple_of(j * tk, tk), tk)]
        return jnp.concatenate([vc, jnp.ones((BF16_ROWS, tk), BF16)], axis=0)

    kc = keys(0)
    for ct in range(n_ct):
        scores(kc, 0, ct)

    def body(jj, c):
        for u in range(unroll):
            j = unroll * jj + u
            kc, va = keys(jnp.minimum(j + 1, n_chunks - 1)), values(j)
            for ct in range(n_ct):
                scores(kc, (u + 1) % 2, ct)
                accumulate(va, u % 2, ct)
        return c

    lax.fori_loop(0, n_chunks // unroll, body, 0)

    lam = (jnp.exp(jnp.sum(lq1_ref[...] * lk1_ref[...], axis=1, keepdims=True))
           - jnp.exp(jnp.sum(lq2_ref[...] * lk2_ref[...], axis=1, keepdims=True)) + LAM_INIT)
    o_all = acc_s[...] / l_s[...]
    o = o_all[:, :tq] - lam * o_all[:, tq:]
    y = o * lax.rsqrt(jnp.mean(o * o, axis=0, keepdims=True) + EPS) * g_ref[...] * (1.0 - LAM_INIT)
    o_ref[0] = y.T.astype(BF16)


def _attn(q3, k3, vt, lam_vecs, subln_g):
    B, S, _ = q3.shape
    tq = _tile(S, 512)
    tk = _tile(S, 512)
    unroll = max(u for u in (2, 4, 8) if (S // tk) % u == 0)
    assert (S // tk) % unroll == 0
    vec = pl.BlockSpec((1, HEAD_DIM), lambda b, h, i: (0, 0))
    kern = functools.partial(_attn_kernel, tq=tq, tk=tk, seq=S, unroll=unroll)
    return pl.pallas_call(
        kern,
        grid=(B, N_HEADS, S // tq),
        in_specs=[vec, vec, vec, vec,
                  pl.BlockSpec((1, tq, V_DIM), lambda b, h, i: (b, i, h)),
                  pl.BlockSpec((1, S, V_DIM), lambda b, h, i: (b, 0, h)),
                  pl.BlockSpec((V_DIM, S), lambda b, h, i: (h, b)),
                  pl.BlockSpec((V_DIM, 1), lambda b, h, i: (0, 0))],
        out_specs=pl.BlockSpec((1, tq, V_DIM), lambda b, h, i: (b, i, h)),
        out_shape=jax.ShapeDtypeStruct((B, S, N_HEADS * V_DIM), BF16),
        scratch_shapes=[pltpu.VMEM((V_DIM, 2 * tq), BF16), pltpu.VMEM((2, 2 * tq // MXU_N, tk, MXU_N), F32),
                        pltpu.VMEM((1, 2 * tq), F32), pltpu.VMEM((1, 2 * tq), F32),
                        pltpu.VMEM((V_DIM, 2 * tq), F32)],
        compiler_params=_cparams(("parallel", "parallel", "arbitrary")),
        name="attn",
    )(*lam_vecs, q3, k3, vt, subln_g)


def _gelu_tanh(x):
    return 0.5 * x * (1.0 + jnp.tanh(math.sqrt(2.0 / math.pi) * (x + 0.044715 * (x * x * x))))


def _merge_kernel(x_ref, hf_ref, hb_ref, gr_ref, ga_ref, gb_ref, yb_ref, wout_ref, g_ref,
                  wrh_ref, wrl_ref, br_ref, h1_ref, xn_ref, lt_ref):
    f = lambda ref: ref[...].astype(F32)
    ya = _gelu_tanh(f(gr_ref)) * (f(hf_ref) + f(hb_ref))
    mixed = _sigmoid(f(ga_ref)) * ya + _sigmoid(f(gb_ref)) * f(yb_ref)
    h1 = x_ref[...] + jnp.dot(mixed.astype(BF16), wout_ref[...], preferred_element_type=F32)
    h1_ref[...] = h1
    xn = _rms(h1, g_ref[...])
    xn_ref[...] = xn
    xn_hi = xn.astype(BF16)
    xn_lo = (xn - xn_hi.astype(F32)).astype(BF16)
    nt = lambda w, v: lax.dot_general(w, v, (((1,), (1,)), ((), ())), preferred_element_type=F32)
    lt_ref[...] = nt(wrh_ref[...], xn_hi) + nt(wrh_ref[...], xn_lo) + nt(wrl_ref[...], xn_hi) + br_ref[...]


def _merge(x2d, hf, hb, gr, ga, gb, yb, w_out, g_ffn, wr_hi, wr_lo, b_router):
    T = x2d.shape[0]
    tm = _tile(T, 512)
    tok = pl.BlockSpec((tm, D_MODEL), lambda i: (i, 0))
    const = lambda shape: pl.BlockSpec(shape, lambda i: (0,) * len(shape))
    return pl.pallas_call(
        _merge_kernel,
        grid=(T // tm,),
        in_specs=[tok] * 7 + [const(w_out.shape), const(g_ffn.shape), const(wr_hi.shape),
                              const(wr_lo.shape), const(b_router.shape)],
        out_specs=(tok, tok, pl.BlockSpec((N_EXPERTS, tm), lambda i: (0, i))),
        out_shape=(jax.ShapeDtypeStruct((T, D_MODEL), F32), jax.ShapeDtypeStruct((T, D_MODEL), F32),
                   jax.ShapeDtypeStruct((N_EXPERTS, T), F32)),
        compiler_params=_cparams(("parallel",)),
        name="merge",
    )(x2d, hf, hb, gr, ga, gb, yb, w_out, g_ffn, wr_hi, wr_lo, b_router)


def _route_kernel(lt_ref, tri_ref, idx_ref, gate_ref, rank_ref, cnt_ref, carry_s, *, tn):
    i = pl.program_id(0)

    @pl.when(i == 0)
    def _():
        carry_s[...] = jnp.zeros_like(carry_s)

    work = lt_ref[...]
    eidx = lax.broadcasted_iota(jnp.int32, work.shape, 0)
    vals, idxs = [], []
    for _ in range(TOP_K):
        mx = jnp.max(work, axis=0, keepdims=True)
        ix = jnp.min(jnp.where(work == mx, eidx, N_EXPERTS), axis=0, keepdims=True)
        vals.append(mx)
        idxs.append(ix)
        work = jnp.where(eidx == ix, -jnp.inf, work)
    ex = [jnp.exp(v - vals[0]) for v in vals]
    denom = ex[0] + ex[1] + ex[2] + ex[3]
    sel = jnp.zeros(work.shape, F32)
    for ix in idxs:
        sel = sel + (eidx == ix).astype(F32)
    before = jnp.dot(sel.astype(BF16), tri_ref[...], preferred_element_type=F32) + carry_s[:, 0:1]
    for k in range(TOP_K):
        rank = jnp.sum(jnp.where(eidx == idxs[k], before, 0.0), axis=0, keepdims=True)
        rank_ref[k:k + 1, :] = rank.astype(jnp.int32)
        idx_ref[k:k + 1, :] = idxs[k]
        gate_ref[k:k + 1, :] = ex[k] / denom
    carry_s[...] = carry_s[...] + jnp.sum(sel, axis=1, keepdims=True)
    cnt_ref[...] = carry_s[...].astype(jnp.int32)


def _route(logits_t):
    E, T = logits_t.shape
    tn = _tile(T, 1024)
    tri = (lax.broadcasted_iota(jnp.int32, (tn, tn), 0) < lax.broadcasted_iota(jnp.int32, (tn, tn), 1)).astype(BF16)
    kt = pl.BlockSpec((TOP_K, tn), lambda i: (0, i))
    kern = functools.partial(_route_kernel, tn=tn)
    return pl.pallas_call(
        kern,
        grid=(T // tn,),
        in_specs=[pl.BlockSpec((E, tn), lambda i: (0, i)), pl.BlockSpec((tn, tn), lambda i: (0, 0))],
        out_specs=(kt, kt, kt, pl.BlockSpec((E, LANES), lambda i: (0, 0))),
        out_shape=(jax.ShapeDtypeStruct((TOP_K, T), jnp.int32), jax.ShapeDtypeStruct((TOP_K, T), F32),
                   jax.ShapeDtypeStruct((TOP_K, T), jnp.int32), jax.ShapeDtypeStruct((E, LANES), jnp.int32)),
        scratch_shapes=[pltpu.VMEM((E, LANES), F32)],
        compiler_params=_cparams(("arbitrary",)),
        name="route",
    )(logits_t, tri)


def _row_copy(src, src_row, dst, dst_row, sem):
    return pltpu.make_async_copy(src.at[pl.ds(src_row, 1)], dst.at[pl.ds(dst_row, 1)], sem)


def _wait_rows(src, dst, n_rows, sem):
    pltpu.make_async_copy(src.at[pl.ds(0, n_rows)], dst.at[pl.ds(0, n_rows)], sem).wait()


def _scatter_kernel(fill_lo_ref, fill_hi_ref, dest_ref, xn_ref, xs_ref, zero_s, sem, zsem, *, tt):
    i = pl.program_id(0)

    @pl.when(i == 0)
    def _():
        zero_s[...] = jnp.zeros_like(zero_s)

        def per_expert(e, n):
            lo, hi = (fill_lo_ref[e] // SUBLANES) * SUBLANES, fill_hi_ref[e]

            def fill(r, c):
                _row_copy(zero_s, 0, xs_ref, r, zsem).start()
                return c

            lax.fori_loop(lo, hi, fill, 0)
            return n + (hi - lo)

        n_fill = lax.fori_loop(0, N_EXPERTS, per_expert, 0)

        @pl.when(n_fill > 0)
        def _():
            n8 = pl.multiple_of(n_fill, SUBLANES)
            pltpu.make_async_copy(xs_ref.at[pl.ds(0, n8)], xs_ref.at[pl.ds(0, n8)], zsem).wait()

    t0 = i * tt

    def body(t, c):
        for k in range(TOP_K):
            _row_copy(xn_ref, t0 + t, xs_ref, dest_ref[t * TOP_K + k], sem).start()
        return c

    lax.fori_loop(0, tt, body, 0, unroll=2)
    _wait_rows(xs_ref, xs_ref, TOP_K * tt, sem)


def _scatter_rows(fill_lo, fill_hi, dest_flat, xn, n_rows):
    T = xn.shape[0]
    tt = _tile(T, 1024)
    kern = functools.partial(_scatter_kernel, tt=tt)
    return pl.pallas_call(
        kern,
        grid_spec=pltpu.PrefetchScalarGridSpec(
            num_scalar_prefetch=2,
            grid=(T // tt,),
            in_specs=[pl.BlockSpec((TOP_K * tt,), lambda i, lo, hi: (i,), memory_space=pltpu.SMEM),
                      pl.BlockSpec(memory_space=pl.ANY)],
            out_specs=pl.BlockSpec(memory_space=pl.ANY),
            scratch_shapes=[pltpu.VMEM((SUBLANES, D_MODEL), F32), pltpu.SemaphoreType.DMA(()),
                            pltpu.SemaphoreType.DMA(())],
        ),
        out_shape=jax.ShapeDtypeStruct((n_rows, D_MODEL), F32),
        compiler_params=_cparams(("arbitrary",)),
        name="scatter",
    )(fill_lo, fill_hi, dest_flat, xn)


def _split_w1_kernel(w_ref, perm_ref, wg_ref, wl_ref):
    z = jnp.dot(w_ref[0].astype(BF16), perm_ref[...], preferred_element_type=F32)
    wg_ref[0] = z[:, :D_MODEL].astype(BF16)
    wl_ref[0] = z[:, D_MODEL:].astype(BF16)


def _split_w1(w1):
    E, D, D2 = w1.shape
    tr = _tile(D, 512)
    col = lax.broadcasted_iota(jnp.int32, (D2, D2), 1)
    row = lax.broadcasted_iota(jnp.int32, (D2, D2), 0)
    perm = (row == jnp.where(col < D2 // 2, 2 * col, 2 * (col - D2 // 2) + 1)).astype(BF16)
    out = pl.BlockSpec((1, tr, D2 // 2), lambda e, i: (e, i, 0))
    return pl.pallas_call(
        _split_w1_kernel,
        grid=(E, D // tr),
        in_specs=[pl.BlockSpec((1, tr, D2), lambda e, i: (e, i, 0)),
                  pl.BlockSpec((D2, D2), lambda e, i: (0, 0), pipeline_mode=pl.Buffered(1))],
        out_specs=(out, out),
        out_shape=(jax.ShapeDtypeStruct((E, D, D2 // 2), BF16),) * 2,
        compiler_params=_cparams(("parallel", "parallel")),
        name="split_w1",
    )(w1, perm)


def _ffn_kernel(be_ref, nu_ref, xs_ref, w1g_ref, w1l_ref, b1g_ref, b1l_ref, w2_ref, b2_ref, ys_ref):
    i = pl.program_id(0)

    @pl.when(i < nu_ref[0])
    def _():
        x = xs_ref[...].astype(BF16)
        hg = jnp.dot(x, w1g_ref[0], preferred_element_type=F32) + b1g_ref[0]
        hl = jnp.dot(x, w1l_ref[0], preferred_element_type=F32) + b1l_ref[0]
        xg = jnp.minimum(hg, SWIGLU_LIMIT)
        xl = jnp.clip(hl, -SWIGLU_LIMIT, SWIGLU_LIMIT)
        act = xg * _sigmoid(SWIGLU_ALPHA * xg) * (xl + 1.0)
        ys_ref[...] = jnp.dot(act.astype(BF16), w2_ref[0], preferred_element_type=F32) + b2_ref[0]

    @pl.when(i >= nu_ref[0])
    def _():
        ys_ref[...] = jnp.zeros_like(ys_ref)


def _ffn(block_e, n_used, xs, w1g, w1l, b1g, b1l, w2, b2):
    n_rows = xs.shape[0]
    n_blocks = n_rows // ROWS_PER_BLOCK
    rows = pl.BlockSpec((ROWS_PER_BLOCK, D_MODEL), lambda i, be, nu: (i, 0))
    rows_in = pl.BlockSpec((ROWS_PER_BLOCK, D_MODEL), lambda i, be, nu: (jnp.minimum(i, nu[0] - 1), 0))
    wspec = pl.BlockSpec((1, D_MODEL, D_MODEL), lambda i, be, nu: (be[i], 0, 0))
    bspec = pl.BlockSpec((1, 1, D_MODEL), lambda i, be, nu: (be[i], 0, 0))
    return pl.pallas_call(
        _ffn_kernel,
        grid_spec=pltpu.PrefetchScalarGridSpec(
            num_scalar_prefetch=2,
            grid=(n_blocks,),
            in_specs=[rows_in, wspec, wspec, bspec, bspec, wspec, bspec],
            out_specs=rows,
        ),
        out_shape=jax.ShapeDtypeStruct((n_rows, D_MODEL), F32),
        compiler_params=_cparams(("arbitrary",)),
        name="ffn",
    )(block_e, n_used, xs, w1g, w1l, b1g, b1l, w2, b2)


def _final_kernel(dest_ref, h1_ref, ys_ref, gate_ref, p_ref, gpl_ref, wpg_ref, wpe_ref, gfin_ref,
                  out_ref, ybuf, sems, *, tm):
    half = tm // 2

    def issue(h):
        def body(t, c):
            tt = h * half + t
            for k in range(TOP_K):
                _row_copy(ys_ref, dest_ref[tt * TOP_K + k], ybuf, k * tm + tt, sems.at[h]).start()
            return c
        lax.fori_loop(0, half, body, 0, unroll=2)

    def finish(h):
        _wait_rows(ys_ref, ybuf, TOP_K * half, sems.at[h])
        rows = pl.ds(h * half, half)
        h2 = h1_ref[rows, :]
        for k in range(TOP_K):
            h2 = h2 + gate_ref[rows, k:k + 1] * ybuf[pl.ds(k * tm + h * half, half), :]
        hn = _rms(h2, gpl_ref[...]).astype(BF16)
        gate = _sigmoid(jnp.dot(hn, wpg_ref[...], preferred_element_type=F32))
        pe = jnp.dot(p_ref[rows, :].astype(BF16), wpe_ref[...], preferred_element_type=F32)
        out_ref[rows, :] = _rms(h2 + gate * pe, gfin_ref[...])

    issue(0)
    issue(1)
    finish(0)
    finish(1)


def _final(dest_flat, h1, ys, gates_tk, p2d, g_pl, w_pg, w_pe, g_final):
    T = h1.shape[0]
    tm = _tile(T, 256)
    tok = pl.BlockSpec((tm, D_MODEL), lambda i: (i, 0))
    const = lambda shape: pl.BlockSpec(shape, lambda i: (0,) * len(shape))
    kern = functools.partial(_final_kernel, tm=tm)
    return pl.pallas_call(
        kern,
        grid=(T // tm,),
        in_specs=[pl.BlockSpec((TOP_K * tm,), lambda i: (i,), memory_space=pltpu.SMEM),
                  tok, pl.BlockSpec(memory_space=pl.ANY),
                  pl.BlockSpec((tm, TOP_K), lambda i: (i, 0)), pl.BlockSpec((tm, PLE_DIM), lambda i: (i, 0)),
                  const(g_pl.shape), const(w_pg.shape), const(w_pe.shape), const(g_final.shape)],
        out_specs=tok,
        out_shape=jax.ShapeDtypeStruct((T, D_MODEL), F32),
        scratch_shapes=[pltpu.VMEM((TOP_K * tm, D_MODEL), F32), pltpu.SemaphoreType.DMA((2,))],
        compiler_params=_cparams(("arbitrary",)),
        name="final",
    )(dest_flat, h1, ys, gates_tk, p2d, g_pl, w_pg, w_pe, g_final)


def _rope_tables(S):
    pos = jnp.arange(S, dtype=F32)
    inv = ROPE_THETA ** (-jnp.arange(0, HEAD_DIM, 2, dtype=F32) / HEAD_DIM)
    ang = pos[:, None] * inv[None, :]
    ang = jnp.concatenate([ang, ang, ang, ang], axis=-1)
    sign = jnp.where((jnp.arange(LANES) % HEAD_DIM) < HEAD_DIM // 2, -1.0, 1.0).astype(F32)
    return jnp.cos(ang), jnp.sin(ang) * sign


def _block_diag_groups(w):
    per = LRU_GROUP // LRU_BLOCK
    g = w.shape[0] // per
    eye = jnp.eye(per, dtype=w.dtype)
    return jnp.einsum('gbij,bc->gbicj', w.reshape(g, per, LRU_BLOCK, LRU_BLOCK), eye).reshape(g, LRU_GROUP, LRU_GROUP)


def _prep_weights(g_mix, w_in, conv_w, conv_b, lru_wa, lru_ba, lru_wx, lru_bx, lru_lambda,
                  lam_q1, lam_k1, lam_q2, lam_k2, subln_g, w_out, g_ffn, w_router, b_router,
                  w1, b1, w2, b2, g_pl, w_pg, w_pe, g_final):
    l = 0
    W = D_MODEL
    w = w_in[l]
    pw = dict(
        g_mix=g_mix[l][None, :],
        w_rest=jnp.concatenate([w[:, :4 * W], w[:, 5 * W:]], axis=1).astype(BF16),
        w_vt=w[:, 4 * W:5 * W].T.astype(BF16),
        conv_w=conv_w[l], conv_b=conv_b[l][None, :],
        lam_vecs=tuple(v[l][None, :] for v in (lam_q1, lam_k1, lam_q2, lam_k2)),
        subln_g=subln_g[l][:, None],
        w_out=w_out[l].astype(BF16), g_ffn=g_ffn[l][None, :],
        b_router=b_router[l][:, None],
        b1g=b1[l][:, None, 0::2], b1l=b1[l][:, None, 1::2],
        w2=w2[l].astype(BF16), b2=b2[l][:, None, :],
        g_pl=g_pl[l][None, :], w_pg=w_pg[l].astype(BF16), w_pe=w_pe[l].astype(BF16),
        g_final=g_final[None, :],
    )
    pw['w1g'], pw['w1l'] = _split_w1(w1[l])
    wr_t = w_router[l].T
    wr_hi = wr_t.astype(BF16)
    pw['wr_hi'] = wr_hi
    pw['wr_lo'] = (wr_t - wr_hi.astype(F32)).astype(BF16)
    for d in range(2):
        wa = _block_diag_groups(lru_wa[l, d])
        wx = _block_diag_groups(lru_wx[l, d])
        pw[f'lru_wg{d}'] = jnp.concatenate([wa, wx], axis=2).astype(BF16)
        ng = D_MODEL // LRU_GROUP
        pw[f'lru_bg{d}'] = jnp.concatenate([lru_ba[l, d].reshape(ng, 1, LRU_GROUP),
                                            lru_bx[l, d].reshape(ng, 1, LRU_GROUP)], axis=2)
        pw[f'lru_lam{d}'] = lru_lambda[l, d][None, :]
    return pw


def _moe_plan(idx, rank, counts, T):
    padded = ((counts + ROWS_PER_BLOCK - 1) // ROWS_PER_BLOCK) * ROWS_PER_BLOCK
    pad_end = jnp.cumsum(padded)
    pad_start = pad_end - padded
    dest = pad_start[idx] + rank
    n_blocks = -(-(T * TOP_K) // ROWS_PER_BLOCK) + N_EXPERTS
    block_start = jnp.arange(n_blocks, dtype=pad_end.dtype) * ROWS_PER_BLOCK
    block_e = jnp.minimum(jnp.sum(pad_end[None, :] <= block_start[:, None], axis=1),
                          N_EXPERTS - 1).astype(jnp.int32)
    n_used = (pad_end[-1:] // ROWS_PER_BLOCK).astype(jnp.int32)
    dest_flat = dest.T.reshape(-1)
    n_rows = n_blocks * ROWS_PER_BLOCK
    fill_lo = (pad_start + counts).astype(jnp.int32)
    fill_hi = pad_end.astype(jnp.int32).at[N_EXPERTS - 1].set(n_rows)
    return dest_flat, fill_lo, fill_hi, block_e, n_used, n_rows


def _trunk(x, p, pw):
    B, S, _ = x.shape
    T = B * S
    x2d = x.reshape(T, D_MODEL)
    cos_t, sin_t = _rope_tables(S)
    xr, gr, q, k, vt, ga, gb = _inproj(x2d, pw['g_mix'], cos_t, sin_t, pw['w_rest'], pw['w_vt'], S)
    xr3 = xr.reshape(B, S, D_MODEL)
    hf = _lru(xr3, pw['conv_w'], pw['conv_b'], pw['lru_wg0'], pw['lru_bg0'], pw['lru_lam0'], False)
    hb = _lru(xr3, pw['conv_w'], pw['conv_b'], pw['lru_wg1'], pw['lru_bg1'], pw['lru_lam1'], True)
    yb = _attn(q.reshape(B, S, D_MODEL), k.reshape(B, S, D_MODEL), vt, pw['lam_vecs'], pw['subln_g'])
    h1, xn, logits_t = _merge(x2d, hf.reshape(T, D_MODEL), hb.reshape(T, D_MODEL), gr, ga, gb,
                              yb.reshape(T, D_MODEL), pw['w_out'], pw['g_ffn'],
                              pw['wr_hi'], pw['wr_lo'], pw['b_router'])
    idx, gates, rank, cnt = _route(logits_t)
    dest_flat, fill_lo, fill_hi, block_e, n_used, n_rows = _moe_plan(idx, rank, cnt[:, 0], T)
    xs = _scatter_rows(fill_lo, fill_hi, dest_flat, xn, n_rows)
    ys = _ffn(block_e, n_used, xs, pw['w1g'], pw['w1l'], pw['b1g'], pw['b1l'], pw['w2'], pw['b2'])
    out = _final(dest_flat, h1, ys, gates.T, p[0].reshape(T, PLE_DIM),
                 pw['g_pl'], pw['w_pg'], pw['w_pe'], pw['g_final'])
    return out.reshape(B, S, D_MODEL)


def kernel(x_prompt, x_sample, p_prompt, p_sample, g_mix, w_in, conv_w, conv_b, lru_wa, lru_ba, lru_wx, lru_bx, lru_lambda, lam_q1, lam_k1, lam_q2, lam_k2, subln_g, w_out, g_ffn, w_router, b_router, w1, b1, w2, b2, g_pl, w_pg, w_pe, g_final):
    pw = _prep_weights(g_mix, w_in, conv_w, conv_b, lru_wa, lru_ba, lru_wx, lru_bx, lru_lambda,
                       lam_q1, lam_k1, lam_q2, lam_k2, subln_g, w_out, g_ffn, w_router, b_router,
                       w1, b1, w2, b2, g_pl, w_pg, w_pe, g_final)
    return (_trunk(x_prompt, p_prompt, pw), _trunk(x_sample, p_sample, pw))
```

```python
import functools
import math

import jax
import jax.numpy as jnp
from jax import lax
from jax.experimental import pallas as pl
from jax.experimental.pallas import tpu as pltpu

F32 = jnp.float32
BF16 = jnp.bfloat16

D_MODEL = 1024
LRU_BLOCK = 64
LRU_C = 8.0
CONV_W = 4
CONV_LEFT = 2
N_HEADS = 8
HEAD_DIM = 64
V_DIM = 128
ROPE_THETA = 10000.0
N_EXPERTS = 32
TOP_K = 4
SWIGLU_LIMIT = 7.0
SWIGLU_ALPHA = 1.702
ROWS_PER_BLOCK = 512
PLE_DIM = 256
EPS = 1e-6
LAM_INIT = 0.8 - 0.6 * math.exp(-0.3 * 0)
LOG2E = 1.4426950408889634

LANES = 128
SUBLANES = 8
BF16_ROWS = 16
MXU_N = 256
LRU_GROUP = MXU_N
VMEM_LIMIT = 56 * 1024 * 1024


def _tile(n, pref):
    t = min(n, pref)
    while n % t:
        t //= 2
    return t


def _cparams(sem):
    return pltpu.CompilerParams(dimension_semantics=sem, vmem_limit_bytes=VMEM_LIMIT)


def _rms(x, g):
    return x * lax.rsqrt(jnp.mean(x * x, axis=-1, keepdims=True) + EPS) * g


def _sigmoid(x):
    return 1.0 / (1.0 + jnp.exp(-x))


def _inproj_kernel(x_ref, g_ref, cos_ref, sin_ref, w_ref, wvt_ref,
                   xr_ref, gr_ref, q_ref, k_ref, vt_ref, ga_ref, gb_ref):
    u = _rms(x_ref[...], g_ref[...]).astype(BF16)

    def proj(seg):
        return jnp.dot(u, w_ref[:, seg * D_MODEL:(seg + 1) * D_MODEL], preferred_element_type=F32)

    reps = D_MODEL // LANES
    cos = jnp.concatenate([cos_ref[...]] * reps, axis=1)
    sin = jnp.concatenate([sin_ref[...]] * reps, axis=1)
    lane = lax.broadcasted_iota(jnp.int32, cos.shape, 1)
    first_half = (lane % HEAD_DIM) < (HEAD_DIM // 2)

    def rope(z):
        swapped = jnp.where(first_half,
                            pltpu.roll(z, D_MODEL - HEAD_DIM // 2, 1),
                            pltpu.roll(z, HEAD_DIM // 2, 1))
        return z * cos + swapped * sin

    xr_ref[...] = proj(0).astype(BF16)
    gr_ref[...] = proj(1).astype(BF16)
    q_ref[...] = (rope(proj(2)) * (HEAD_DIM ** -0.5 * LOG2E)).astype(BF16)
    k_ref[...] = rope(proj(3)).astype(BF16)
    ga_ref[...] = proj(4).astype(BF16)
    gb_ref[...] = proj(5).astype(BF16)
    vt = lax.dot_general(wvt_ref[...], u, (((1,), (1,)), ((), ())), preferred_element_type=F32)
    vt_ref[...] = vt.astype(BF16)


def _inproj(x2d, g_mix, cos_t, sin_t, w_rest, w_vt, seq):
    T = x2d.shape[0]
    tm = _tile(seq, 512)
    n_pos = seq // tm
    tok = pl.BlockSpec((tm, D_MODEL), lambda i: (i, 0))
    pos = pl.BlockSpec((tm, LANES), lambda i: (i % n_pos, 0))
    const = lambda shape: pl.BlockSpec(shape, lambda i: (0,) * len(shape), pipeline_mode=pl.Buffered(1))
    outs = [jax.ShapeDtypeStruct((T, D_MODEL), BF16)] * 4
    out_shape = (outs[0], outs[1], outs[2], outs[3],
                 jax.ShapeDtypeStruct((D_MODEL, T), BF16), outs[0], outs[0])
    return pl.pallas_call(
        _inproj_kernel,
        grid=(T // tm,),
        in_specs=[tok, const((1, D_MODEL)), pos, pos,
                  const(w_rest.shape), const(w_vt.shape)],
        out_specs=(tok, tok, tok, tok, pl.BlockSpec((D_MODEL, tm), lambda i: (0, i)), tok, tok),
        out_shape=out_shape,
        compiler_params=_cparams(("parallel",)),
        name="inproj",
    )(x2d, g_mix, cos_t, sin_t, w_rest, w_vt)


def _lru_kernel(xm_ref, xp_ref, xn_ref, cw_ref, cb_ref, wg_ref, bg_ref, lam_ref,
                h_ref, a_s, b_s, carry_s, *, reverse, tm, n_tiles):
    i = pl.program_id(1)
    ti = (n_tiles - 1 - i) if reverse else i
    C = D_MODEL

    @pl.when(i == 0)
    def _():
        carry_s[...] = jnp.zeros_like(carry_s)

    xm = xm_ref[0].astype(F32)
    xp = xp_ref[0].astype(F32) * (ti > 0).astype(F32)
    xn = xn_ref[0].astype(F32) * (ti < n_tiles - 1).astype(F32)
    big = jnp.concatenate([xp, xm, xn], axis=0)
    xc = jnp.broadcast_to(cb_ref[...], (tm, C))
    for j in range(CONV_W):
        off = BF16_ROWS + j - CONV_LEFT
        xc = xc + big[off:off + tm, :] * cw_ref[j:j + 1, :]

    xcb = xc.astype(BF16)
    r_parts, i_parts = [], []
    for g in range(C // LRU_GROUP):
        z = jnp.dot(xcb[:, g * LRU_GROUP:(g + 1) * LRU_GROUP], wg_ref[g], preferred_element_type=F32)
        z = z + bg_ref[g]
        r_parts.append(z[:, :LRU_GROUP])
        i_parts.append(z[:, LRU_GROUP:])
    r = _sigmoid(jnp.concatenate(r_parts, axis=1))
    ig = _sigmoid(jnp.concatenate(i_parts, axis=1))

    neg_lam = -lam_ref[...]
    softplus = jnp.maximum(neg_lam, 0.0) + jnp.log1p(jnp.exp(-jnp.abs(neg_lam)))
    a = jnp.exp((-LRU_C * softplus) * r)
    y = 1.0 - a * a
    a_s[...] = a
    b_s[...] = (y * lax.rsqrt(jnp.maximum(y, 1e-30))) * (ig * xc)

    n_slab = tm // SUBLANES
    row = lax.broadcasted_iota(jnp.int32, (SUBLANES, C), 0)

    def body(s, carry):
        sl = (n_slab - 1 - s) if reverse else s
        r0 = pl.multiple_of(sl * SUBLANES, SUBLANES)
        a8 = a_s[pl.ds(r0, SUBLANES), :]
        b8 = b_s[pl.ds(r0, SUBLANES), :]
        for d in (1, 2, 4):
            if reverse:
                valid = row < SUBLANES - d
                shift = SUBLANES - d
            else:
                valid = row >= d
                shift = d
            a_sh = jnp.where(valid, pltpu.roll(a8, shift, 0), 1.0)
            b_sh = jnp.where(valid, pltpu.roll(b8, shift, 0), 0.0)
            b8 = a8 * b_sh + b8
            a8 = a8 * a_sh
        h8 = b8 + a8 * carry
        b_s[pl.ds(r0, SUBLANES), :] = h8
        edge = h8[0:1, :] if reverse else h8[SUBLANES - 1:SUBLANES, :]
        return jnp.broadcast_to(edge, (SUBLANES, C))

    carry_s[...] = lax.fori_loop(0, n_slab, body, carry_s[...])
    h_ref[0] = b_s[...].astype(BF16)


def _lru(xr3, conv_w, conv_b, wg, bg, lam, reverse):
    B, S, C = xr3.shape
    tm = _tile(S, 256)
    n_tiles = S // tm
    hb = tm // BF16_ROWS
    n_halo = S // BF16_ROWS
    tidx = (lambda i: n_tiles - 1 - i) if reverse else (lambda i: i)
    const = lambda shape: pl.BlockSpec(shape, lambda b, i: (0,) * len(shape))
    kern = functools.partial(_lru_kernel, reverse=reverse, tm=tm, n_tiles=n_tiles)
    return pl.pallas_call(
        kern,
        grid=(B, n_tiles),
        in_specs=[
            pl.BlockSpec((1, tm, C), lambda b, i: (b, tidx(i), 0)),
            pl.BlockSpec((1, BF16_ROWS, C), lambda b, i: (b, jnp.maximum(tidx(i) * hb - 1, 0), 0)),
            pl.BlockSpec((1, BF16_ROWS, C), lambda b, i: (b, jnp.minimum((tidx(i) + 1) * hb, n_halo - 1), 0)),
            const(conv_w.shape), const(conv_b.shape), const(wg.shape), const(bg.shape), const(lam.shape),
        ],
        out_specs=pl.BlockSpec((1, tm, C), lambda b, i: (b, tidx(i), 0)),
        out_shape=jax.ShapeDtypeStruct((B, S, C), BF16),
        scratch_shapes=[pltpu.VMEM((tm, C), F32), pltpu.VMEM((tm, C), F32), pltpu.VMEM((SUBLANES, C), F32)],
        compiler_params=_cparams(("parallel", "arbitrary")),
        name="lru_bwd" if reverse else "lru_fwd",
    )(xr3, xr3, xr3, conv_w, conv_b, wg, bg, lam)


def _attn_kernel(lq1_ref, lk1_ref, lq2_ref, lk2_ref, q_ref, k_ref, vt_ref, g_ref,
                 o_ref, qz_s, s_s, m_s, l_s, acc_s, *, tq, tk, seq, unroll):
    qt = q_ref[0].astype(F32).T
    chan = lax.broadcasted_iota(jnp.int32, qt.shape, 0)
    qz_s[:, 0:tq] = jnp.where(chan < HEAD_DIM, qt, 0.0).astype(BF16)
    qz_s[:, tq:2 * tq] = jnp.where(chan >= HEAD_DIM, qt, 0.0).astype(BF16)
    m_s[...] = jnp.full_like(m_s, -1e30)
    l_s[...] = jnp.zeros_like(l_s)
    acc_s[...] = jnp.zeros_like(acc_s)
    n_chunks = seq // tk

    n_ct = 2 * tq // MXU_N

    def scores(kc, slot, ct):
        cols = slice(ct * MXU_N, (ct + 1) * MXU_N)
        s_s[slot, ct] = jnp.dot(kc, qz_s[:, cols], preferred_element_type=F32)

    def accumulate(va, slot, ct):
        cols = slice(ct * MXU_N, (ct + 1) * MXU_N)
        s = s_s[slot, ct]
        m_old = m_s[:, cols]
        m_new = jnp.maximum(m_old, jnp.max(s, axis=0, keepdims=True))
        alpha = jnp.exp2(m_old - m_new)
        p = jnp.exp2(s - m_new)
        pv = jnp.dot(va, p.astype(BF16), preferred_element_type=F32)
        l_s[:, cols] = alpha * l_s[:, cols] + pv[V_DIM:V_DIM + 1, :]
        acc_s[:, cols] = alpha * acc_s[:, cols] + pv[:V_DIM, :]
        m_s[:, cols] = m_new

    def keys(j):
        return k_ref[0, pl.ds(pl.multiple_of(j * tk, tk), tk), :]

    def values(j):
        vc = vt_ref[:, pl.ds(pl.multiple_of(j * tk, tk), tk)]
        return jnp.concatenate([vc, jnp.ones((BF16_ROWS, tk), BF16)], axis=0)

    kc = keys(0)
    for ct in range(n_ct):
        scores(kc, 0, ct)

    def body(jj, c):
        for u in range(unroll):
            j = unroll * jj + u
            kc, va = keys(jnp.minimum(j + 1, n_chunks - 1)), values(j)
            for ct in range(n_ct):
                scores(kc, (u + 1) % 2, ct)
                accumulate(va, u % 2, ct)
        return c

    lax.fori_loop(0, n_chunks // unroll, body, 0)

    lam = (jnp.exp(jnp.sum(lq1_ref[...] * lk1_ref[...], axis=1, keepdims=True))
           - jnp.exp(jnp.sum(lq2_ref[...] * lk2_ref[...], axis=1, keepdims=True)) + LAM_INIT)
    o_all = acc_s[...] / l_s[...]
    o = o_all[:, :tq] - lam * o_all[:, tq:]
    y = o * lax.rsqrt(jnp.mean(o * o, axis=0, keepdims=True) + EPS) * g_ref[...] * (1.0 - LAM_INIT)
    o_ref[0] = y.T.astype(BF16)


def _attn(q3, k3, vt, lam_vecs, subln_g):
    B, S, _ = q3.shape
    tq = _tile(S, 512)
    tk = _tile(S, 512)
    unroll = max(u for u in (2, 4, 8) if (S // tk) % u == 0)
    assert (S // tk) % unroll == 0
    vec = pl.BlockSpec((1, HEAD_DIM), lambda b, h, i: (0, 0))
    kern = functools.partial(_attn_kernel, tq=tq, tk=tk, seq=S, unroll=unroll)
    return pl.pallas_call(
        kern,
        grid=(B, N_HEADS, S // tq),
        in_specs=[vec, vec, vec, vec,
                  pl.BlockSpec((1, tq, V_DIM), lambda b, h, i: (b, i, h)),
                  pl.BlockSpec((1, S, V_DIM), lambda b, h, i: (b, 0, h)),
                  pl.BlockSpec((V_DIM, S), lambda b, h, i: (h, b)),
                  pl.BlockSpec((V_DIM, 1), lambda b, h, i: (0, 0))],
        out_specs=pl.BlockSpec((1, tq, V_DIM), lambda b, h, i: (b, i, h)),
        out_shape=jax.ShapeDtypeStruct((B, S, N_HEADS * V_DIM), BF16),
        scratch_shapes=[pltpu.VMEM((V_DIM, 2 * tq), BF16), pltpu.VMEM((2, 2 * tq // MXU_N, tk, MXU_N), F32),
                        pltpu.VMEM((1, 2 * tq), F32), pltpu.VMEM((1, 2 * tq), F32),
                        pltpu.VMEM((V_DIM, 2 * tq), F32)],
        compiler_params=_cparams(("parallel", "parallel", "arbitrary")),
        name="attn",
    )(*lam_vecs, q3, k3, vt, subln_g)


def _gelu_tanh(x):
    return 0.5 * x * (1.0 + jnp.tanh(math.sqrt(2.0 / math.pi) * (x + 0.044715 * (x * x * x))))


def _merge_kernel(x_ref, hf_ref, hb_ref, gr_ref, ga_ref, gb_ref, yb_ref, wout_ref, g_ref,
                  wrh_ref, wrl_ref, br_ref, h1_ref, xn_ref, lt_ref):
    f = lambda ref: ref[...].astype(F32)
    ya = _gelu_tanh(f(gr_ref)) * (f(hf_ref) + f(hb_ref))
    mixed = _sigmoid(f(ga_ref)) * ya + _sigmoid(f(gb_ref)) * f(yb_ref)
    h1 = x_ref[...] + jnp.dot(mixed.astype(BF16), wout_ref[...], preferred_element_type=F32)
    h1_ref[...] = h1
    xn = _rms(h1, g_ref[...])
    xn_ref[...] = xn
    xn_hi = xn.astype(BF16)
    xn_lo = (xn - xn_hi.astype(F32)).astype(BF16)
    nt = lambda w, v: lax.dot_general(w, v, (((1,), (1,)), ((), ())), preferred_element_type=F32)
    lt_ref[...] = nt(wrh_ref[...], xn_hi) + nt(wrh_ref[...], xn_lo) + nt(wrl_ref[...], xn_hi) + br_ref[...]


def _merge(x2d, hf, hb, gr, ga, gb, yb, w_out, g_ffn, wr_hi, wr_lo, b_router):
    T = x2d.shape[0]
    tm = _tile(T, 512)
    tok = pl.BlockSpec((tm, D_MODEL), lambda i: (i, 0))
    const = lambda shape: pl.BlockSpec(shape, lambda i: (0,) * len(shape))
    return pl.pallas_call(
        _merge_kernel,
        grid=(T // tm,),
        in_specs=[tok] * 7 + [const(w_out.shape), const(g_ffn.shape), const(wr_hi.shape),
                              const(wr_lo.shape), const(b_router.shape)],
        out_specs=(tok, tok, pl.BlockSpec((N_EXPERTS, tm), lambda i: (0, i))),
        out_shape=(jax.ShapeDtypeStruct((T, D_MODEL), F32), jax.ShapeDtypeStruct((T, D_MODEL), F32),
                   jax.ShapeDtypeStruct((N_EXPERTS, T), F32)),
        compiler_params=_cparams(("parallel",)),
        name="merge",
    )(x2d, hf, hb, gr, ga, gb, yb, w_out, g_ffn, wr_hi, wr_lo, b_router)


def _route_kernel(lt_ref, tri_ref, idx_ref, gate_ref, rank_ref, cnt_ref, carry_s, *, tn):
    i = pl.program_id(0)

    @pl.when(i == 0)
    def _():
        carry_s[...] = jnp.zeros_like(carry_s)

    work = lt_ref[...]
    eidx = lax.broadcasted_iota(jnp.int32, work.shape, 0)
    vals, idxs = [], []
    for _ in range(TOP_K):
        mx = jnp.max(work, axis=0, keepdims=True)
        ix = jnp.min(jnp.where(work == mx, eidx, N_EXPERTS), axis=0, keepdims=True)
        vals.append(mx)
        idxs.append(ix)
        work = jnp.where(eidx == ix, -jnp.inf, work)
    ex = [jnp.exp(v - vals[0]) for v in vals]
    denom = ex[0] + ex[1] + ex[2] + ex[3]
    sel = jnp.zeros(work.shape, F32)
    for ix in idxs:
        sel = sel + (eidx == ix).astype(F32)
    before = jnp.dot(sel.astype(BF16), tri_ref[...], preferred_element_type=F32) + carry_s[:, 0:1]
    for k in range(TOP_K):
        rank = jnp.sum(jnp.where(eidx == idxs[k], before, 0.0), axis=0, keepdims=True)
        rank_ref[k:k + 1, :] = rank.astype(jnp.int32)
        idx_ref[k:k + 1, :] = idxs[k]
        gate_ref[k:k + 1, :] = ex[k] / denom
    carry_s[...] = carry_s[...] + jnp.sum(sel, axis=1, keepdims=True)
    cnt_ref[...] = carry_s[...].astype(jnp.int32)


def _route(logits_t):
    E, T = logits_t.shape
    tn = _tile(T, 1024)
    tri = (lax.broadcasted_iota(jnp.int32, (tn, tn), 0) < lax.broadcasted_iota(jnp.int32, (tn, tn), 1)).astype(BF16)
    kt = pl.BlockSpec((TOP_K, tn), lambda i: (0, i))
    kern = functools.partial(_route_kernel, tn=tn)
    return pl.pallas_call(
        kern,
        grid=(T // tn,),
        in_specs=[pl.BlockSpec((E, tn), lambda i: (0, i)), pl.BlockSpec((tn, tn), lambda i: (0, 0))],
        out_specs=(kt, kt, kt, pl.BlockSpec((E, LANES), lambda i: (0, 0))),
        out_shape=(jax.ShapeDtypeStruct((TOP_K, T), jnp.int32), jax.ShapeDtypeStruct((TOP_K, T), F32),
                   jax.ShapeDtypeStruct((TOP_K, T), jnp.int32), jax.ShapeDtypeStruct((E, LANES), jnp.int32)),
        scratch_shapes=[pltpu.VMEM((E, LANES), F32)],
        compiler_params=_cparams(("arbitrary",)),
        name="route",
    )(logits_t, tri)


def _row_copy(src, src_row, dst, dst_row, sem):
    return pltpu.make_async_copy(src.at[pl.ds(src_row, 1)], dst.at[pl.ds(dst_row, 1)], sem)


def _wait_rows(src, dst, n_rows, sem):
    pltpu.make_async_copy(src.at[pl.ds(0, n_rows)], dst.at[pl.ds(0, n_rows)], sem).wait()


def _scatter_kernel(fill_lo_ref, fill_hi_ref, dest_ref, xn_ref, xs_ref, zero_s, sem, zsem, *, tt):
    i = pl.program_id(0)

    @pl.when(i == 0)
    def _():
        zero_s[...] = jnp.zeros_like(zero_s)

        def per_expert(e, n):
            lo, hi = (fill_lo_ref[e] // SUBLANES) * SUBLANES, fill_hi_ref[e]

            def fill(r, c):
                _row_copy(zero_s, 0, xs_ref, r, zsem).start()
                return c

            lax.fori_loop(lo, hi, fill, 0)
            return n + (hi - lo)

        n_fill = lax.fori_loop(0, N_EXPERTS, per_expert, 0)

        @pl.when(n_fill > 0)
        def _():
            n8 = pl.multiple_of(n_fill, SUBLANES)
            pltpu.make_async_copy(xs_ref.at[pl.ds(0, n8)], xs_ref.at[pl.ds(0, n8)], zsem).wait()

    def body(t, c):
        for k in range(TOP_K):
            _row_copy(xn_ref, t, xs_ref, dest_ref[t * TOP_K + k], sem).start()
        return c

    lax.fori_loop(0, tt, body, 0, unroll=2)
    _wait_rows(xs_ref, xs_ref, TOP_K * tt, sem)


def _scatter_rows(fill_lo, fill_hi, dest_flat, xn, n_rows):
    T = xn.shape[0]
    tt = _tile(T, 512)
    kern = functools.partial(_scatter_kernel, tt=tt)
    return pl.pallas_call(
        kern,
        grid_spec=pltpu.PrefetchScalarGridSpec(
            num_scalar_prefetch=2,
            grid=(T // tt,),
            in_specs=[pl.BlockSpec((TOP_K * tt,), lambda i, lo, hi: (i,), memory_space=pltpu.SMEM),
                      pl.BlockSpec((tt, D_MODEL), lambda i, lo, hi: (i, 0))],
            out_specs=pl.BlockSpec(memory_space=pl.ANY),
            scratch_shapes=[pltpu.VMEM((SUBLANES, D_MODEL), F32), pltpu.SemaphoreType.DMA(()),
                            pltpu.SemaphoreType.DMA(())],
        ),
        out_shape=jax.ShapeDtypeStruct((n_rows, D_MODEL), F32),
        compiler_params=_cparams(("arbitrary",)),
        name="scatter",
    )(fill_lo, fill_hi, dest_flat, xn)


def _split_w1_kernel(w_ref, perm_ref, wg_ref, wl_ref):
    z = jnp.dot(w_ref[0].astype(BF16), perm_ref[...], preferred_element_type=F32)
    wg_ref[0] = z[:, :D_MODEL].astype(BF16)
    wl_ref[0] = z[:, D_MODEL:].astype(BF16)


def _split_w1(w1):
    E, D, D2 = w1.shape
    tr = _tile(D, 512)
    col = lax.broadcasted_iota(jnp.int32, (D2, D2), 1)
    row = lax.broadcasted_iota(jnp.int32, (D2, D2), 0)
    perm = (row == jnp.where(col < D2 // 2, 2 * col, 2 * (col - D2 // 2) + 1)).astype(BF16)
    out = pl.BlockSpec((1, tr, D2 // 2), lambda e, i: (e, i, 0))
    return pl.pallas_call(
        _split_w1_kernel,
        grid=(E, D // tr),
        in_specs=[pl.BlockSpec((1, tr, D2), lambda e, i: (e, i, 0)),
                  pl.BlockSpec((D2, D2), lambda e, i: (0, 0), pipeline_mode=pl.Buffered(1))],
        out_specs=(out, out),
        out_shape=(jax.ShapeDtypeStruct((E, D, D2 // 2), BF16),) * 2,
        compiler_params=_cparams(("parallel", "parallel")),
        name="split_w1",
    )(w1, perm)


def _ffn_kernel(be_ref, nu_ref, xs_ref, w1g_ref, w1l_ref, b1g_ref, b1l_ref, w2_ref, b2_ref, ys_ref):
    i = pl.program_id(0)

    @pl.when(i < nu_ref[0])
    def _():
        x = xs_ref[...].astype(BF16)
        hg = jnp.dot(x, w1g_ref[0], preferred_element_type=F32) + b1g_ref[0]
        hl = jnp.dot(x, w1l_ref[0], preferred_element_type=F32) + b1l_ref[0]
        xg = jnp.minimum(hg, SWIGLU_LIMIT)
        xl = jnp.clip(hl, -SWIGLU_LIMIT, SWIGLU_LIMIT)
        act = xg * _sigmoid(SWIGLU_ALPHA * xg) * (xl + 1.0)
        ys_ref[...] = jnp.dot(act.astype(BF16), w2_ref[0], preferred_element_type=F32) + b2_ref[0]

    @pl.when(i >= nu_ref[0])
    def _():
        ys_ref[...] = jnp.zeros_like(ys_ref)


def _ffn(block_e, n_used, xs, w1g, w1l, b1g, b1l, w2, b2):
    n_rows = xs.shape[0]
    n_blocks = n_rows // ROWS_PER_BLOCK
    rows = pl.BlockSpec((ROWS_PER_BLOCK, D_MODEL), lambda i, be, nu: (i, 0))
    rows_in = pl.BlockSpec((ROWS_PER_BLOCK, D_MODEL), lambda i, be, nu: (jnp.minimum(i, nu[0] - 1), 0))
    wspec = pl.BlockSpec((1, D_MODEL, D_MODEL), lambda i, be, nu: (be[i], 0, 0))
    bspec = pl.BlockSpec((1, 1, D_MODEL), lambda i, be, nu: (be[i], 0, 0))
    return pl.pallas_call(
        _ffn_kernel,
        grid_spec=pltpu.PrefetchScalarGridSpec(
            num_scalar_prefetch=2,
            grid=(n_blocks,),
            in_specs=[rows_in, wspec, wspec, bspec, bspec, wspec, bspec],
            out_specs=rows,
        ),
        out_shape=jax.ShapeDtypeStruct((n_rows, D_MODEL), F32),
        compiler_params=_cparams(("arbitrary",)),
        name="ffn",
    )(block_e, n_used, xs, w1g, w1l, b1g, b1l, w2, b2)


def _final_kernel(dest_ref, h1_ref, ys_ref, gate_ref, p_ref, gpl_ref, wpg_ref, wpe_ref, gfin_ref,
                  out_ref, ybuf, sems, *, tm):
    half = tm // 2

    def issue(h):
        def body(t, c):
            tt = h * half + t
            for k in range(TOP_K):
                _row_copy(ys_ref, dest_ref[tt * TOP_K + k], ybuf, k * tm + tt, sems.at[h]).start()
            return c
        lax.fori_loop(0, half, body, 0, unroll=2)

    def finish(h):
        _wait_rows(ys_ref, ybuf, TOP_K * half, sems.at[h])
        rows = pl.ds(h * half, half)
        h2 = h1_ref[rows, :]
        for k in range(TOP_K):
            h2 = h2 + gate_ref[rows, k:k + 1] * ybuf[pl.ds(k * tm + h * half, half), :]
        hn = _rms(h2, gpl_ref[...]).astype(BF16)
        gate = _sigmoid(jnp.dot(hn, wpg_ref[...], preferred_element_type=F32))
        pe = jnp.dot(p_ref[rows, :].astype(BF16), wpe_ref[...], preferred_element_type=F32)
        out_ref[rows, :] = _rms(h2 + gate * pe, gfin_ref[...])

    issue(0)
    issue(1)
    finish(0)
    finish(1)


def _final(dest_flat, h1, ys, gates_tk, p2d, g_pl, w_pg, w_pe, g_final):
    T = h1.shape[0]
    tm = _tile(T, 256)
    tok = pl.BlockSpec((tm, D_MODEL), lambda i: (i, 0))
    const = lambda shape: pl.BlockSpec(shape, lambda i: (0,) * len(shape))
    kern = functools.partial(_final_kernel, tm=tm)
    return pl.pallas_call(
        kern,
        grid=(T // tm,),
        in_specs=[pl.BlockSpec((TOP_K * tm,), lambda i: (i,), memory_space=pltpu.SMEM),
                  tok, pl.BlockSpec(memory_space=pl.ANY),
                  pl.BlockSpec((tm, TOP_K), lambda i: (i, 0)), pl.BlockSpec((tm, PLE_DIM), lambda i: (i, 0)),
                  const(g_pl.shape), const(w_pg.shape), const(w_pe.shape), const(g_final.shape)],
        out_specs=tok,
        out_shape=jax.ShapeDtypeStruct((T, D_MODEL), F32),
        scratch_shapes=[pltpu.VMEM((TOP_K * tm, D_MODEL), F32), pltpu.SemaphoreType.DMA((2,))],
        compiler_params=_cparams(("arbitrary",)),
        name="final",
    )(dest_flat, h1, ys, gates_tk, p2d, g_pl, w_pg, w_pe, g_final)


def _rope_tables(S):
    pos = jnp.arange(S, dtype=F32)
    inv = ROPE_THETA ** (-jnp.arange(0, HEAD_DIM, 2, dtype=F32) / HEAD_DIM)
    ang = pos[:, None] * inv[None, :]
    ang = jnp.concatenate([ang, ang, ang, ang], axis=-1)
    sign = jnp.where((jnp.arange(LANES) % HEAD_DIM) < HEAD_DIM // 2, -1.0, 1.0).astype(F32)
    return jnp.cos(ang), jnp.sin(ang) * sign


def _block_diag_groups(w):
    per = LRU_GROUP // LRU_BLOCK
    g = w.shape[0] // per
    eye = jnp.eye(per, dtype=w.dtype)
    return jnp.einsum('gbij,bc->gbicj', w.reshape(g, per, LRU_BLOCK, LRU_BLOCK), eye).reshape(g, LRU_GROUP, LRU_GROUP)


def _prep_weights(g_mix, w_in, conv_w, conv_b, lru_wa, lru_ba, lru_wx, lru_bx, lru_lambda,
                  lam_q1, lam_k1, lam_q2, lam_k2, subln_g, w_out, g_ffn, w_router, b_router,
                  w1, b1, w2, b2, g_pl, w_pg, w_pe, g_final):
    l = 0
    W = D_MODEL
    w = w_in[l]
    pw = dict(
        g_mix=g_mix[l][None, :],
        w_rest=jnp.concatenate([w[:, :4 * W], w[:, 5 * W:]], axis=1).astype(BF16),
        w_vt=w[:, 4 * W:5 * W].T.astype(BF16),
        conv_w=conv_w[l], conv_b=conv_b[l][None, :],
        lam_vecs=tuple(v[l][None, :] for v in (lam_q1, lam_k1, lam_q2, lam_k2)),
        subln_g=subln_g[l][:, None],
        w_out=w_out[l].astype(BF16), g_ffn=g_ffn[l][None, :],
        b_router=b_router[l][:, None],
        b1g=b1[l][:, None, 0::2], b1l=b1[l][:, None, 1::2],
        w2=w2[l].astype(BF16), b2=b2[l][:, None, :],
        g_pl=g_pl[l][None, :], w_pg=w_pg[l].astype(BF16), w_pe=w_pe[l].astype(BF16),
        g_final=g_final[None, :],
    )
    pw['w1g'], pw['w1l'] = _split_w1(w1[l])
    wr_t = w_router[l].T
    wr_hi = wr_t.astype(BF16)
    pw['wr_hi'] = wr_hi
    pw['wr_lo'] = (wr_t - wr_hi.astype(F32)).astype(BF16)
    for d in range(2):
        wa = _block_diag_groups(lru_wa[l, d])
        wx = _block_diag_groups(lru_wx[l, d])
        pw[f'lru_wg{d}'] = jnp.concatenate([wa, wx], axis=2).astype(BF16)
        ng = D_MODEL // LRU_GROUP
        pw[f'lru_bg{d}'] = jnp.concatenate([lru_ba[l, d].reshape(ng, 1, LRU_GROUP),
                                            lru_bx[l, d].reshape(ng, 1, LRU_GROUP)], axis=2)
        pw[f'lru_lam{d}'] = lru_lambda[l, d][None, :]
    return pw


def _moe_plan(idx, rank, counts, T):
    padded = ((counts + ROWS_PER_BLOCK - 1) // ROWS_PER_BLOCK) * ROWS_PER_BLOCK
    pad_end = jnp.cumsum(padded)
    pad_start = pad_end - padded
    onehot = idx[..., None] == jnp.arange(N_EXPERTS, dtype=idx.dtype)
    dest = jnp.sum(jnp.where(onehot, pad_start.astype(idx.dtype), 0), axis=-1) + rank
    n_blocks = -(-(T * TOP_K) // ROWS_PER_BLOCK) + N_EXPERTS
    block_start = jnp.arange(n_blocks, dtype=pad_end.dtype) * ROWS_PER_BLOCK
    block_e = jnp.minimum(jnp.sum(pad_end[None, :] <= block_start[:, None], axis=1),
                          N_EXPERTS - 1).astype(jnp.int32)
    n_used = (pad_end[-1:] // ROWS_PER_BLOCK).astype(jnp.int32)
    dest_flat = dest.T.reshape(-1)
    n_rows = n_blocks * ROWS_PER_BLOCK
    fill_lo = (pad_start + counts).astype(jnp.int32)
    fill_hi = pad_end.astype(jnp.int32).at[N_EXPERTS - 1].set(n_rows)
    return dest_flat, fill_lo, fill_hi, block_e, n_used, n_rows


def _trunk(x, p, pw):
    B, S, _ = x.shape
    T = B * S
    x2d = x.reshape(T, D_MODEL)
    cos_t, sin_t = _rope_tables(S)
    xr, gr, q, k, vt, ga, gb = _inproj(x2d, pw['g_mix'], cos_t, sin_t, pw['w_rest'], pw['w_vt'], S)
    xr3 = xr.reshape(B, S, D_MODEL)
    hf = _lru(xr3, pw['conv_w'], pw['conv_b'], pw['lru_wg0'], pw['lru_bg0'], pw['lru_lam0'], False)
    hb = _lru(xr3, pw['conv_w'], pw['conv_b'], pw['lru_wg1'], pw['lru_bg1'], pw['lru_lam1'], True)
    yb = _attn(q.reshape(B, S, D_MODEL), k.reshape(B, S, D_MODEL), vt, pw['lam_vecs'], pw['subln_g'])
    h1, xn, logits_t = _merge(x2d, hf.reshape(T, D_MODEL), hb.reshape(T, D_MODEL), gr, ga, gb,
                              yb.reshape(T, D_MODEL), pw['w_out'], pw['g_ffn'],
                              pw['wr_hi'], pw['wr_lo'], pw['b_router'])
    idx, gates, rank, cnt = _route(logits_t)
    dest_flat, fill_lo, fill_hi, block_e, n_used, n_rows = _moe_plan(idx, rank, cnt[:, 0], T)
    xs = _scatter_rows(fill_lo, fill_hi, dest_flat, xn, n_rows)
    ys = _ffn(block_e, n_used, xs, pw['w1g'], pw['w1l'], pw['b1g'], pw['b1l'], pw['w2'], pw['b2'])
    out = _final(dest_flat, h1, ys, gates.T, p[0].reshape(T, PLE_DIM),
                 pw['g_pl'], pw['w_pg'], pw['w_pe'], pw['g_final'])
    return out.reshape(B, S, D_MODEL)


def kernel(x_prompt, x_sample, p_prompt, p_sample, g_mix, w_in, conv_w, conv_b, lru_wa, lru_ba, lru_wx, lru_bx, lru_lambda, lam_q1, lam_k1, lam_q2, lam_k2, subln_g, w_out, g_ffn, w_router, b_router, w1, b1, w2, b2, g_pl, w_pg, w_pe, g_final):
    pw = _prep_weights(g_mix, w_in, conv_w, conv_b, lru_wa, lru_ba, lru_wx, lru_bx, lru_lambda,
                       lam_q1, lam_k1, lam_q2, lam_k2, subln_g, w_out, g_ffn, w_router, b_router,
                       w1, b1, w2, b2, g_pl, w_pg, w_pe, g_final)
    return (_trunk(x_prompt, p_prompt, pw), _trunk(x_sample, p_sample, pw))
```

```python
import functools
import math

import jax
import jax.numpy as jnp
from jax import lax
from jax.experimental import pallas as pl
from jax.experimental.pallas import tpu as pltpu

F32 = jnp.float32
BF16 = jnp.bfloat16

D_MODEL = 1024
LRU_BLOCK = 64
LRU_C = 8.0
CONV_W = 4
CONV_LEFT = 2
N_HEADS = 8
HEAD_DIM = 64
V_DIM = 128
ROPE_THETA = 10000.0
N_EXPERTS = 32
TOP_K = 4
SWIGLU_LIMIT = 7.0
SWIGLU_ALPHA = 1.702
ROWS_PER_BLOCK = 512
PLE_DIM = 256
EPS = 1e-6
LAM_INIT = 0.8 - 0.6 * math.exp(-0.3 * 0)
LOG2E = 1.4426950408889634

LANES = 128
SUBLANES = 8
BF16_ROWS = 16
MXU_N = 256
LRU_GROUP = MXU_N
VMEM_LIMIT = 56 * 1024 * 1024


def _tile(n, pref):
    t = min(n, pref)
    while n % t:
        t //= 2
    return t


def _cparams(sem):
    return pltpu.CompilerParams(dimension_semantics=sem, vmem_limit_bytes=VMEM_LIMIT)


def _rms(x, g):
    return x * lax.rsqrt(jnp.mean(x * x, axis=-1, keepdims=True) + EPS) * g


def _sigmoid(x):
    return 1.0 / (1.0 + jnp.exp(-x))


ROW_TILES = D_MODEL // LANES


def _tile_rows_shape(n):
    return jax.ShapeDtypeStruct((n * ROW_TILES, LANES), F32)


def _tile_rows_spec(n, index_map):
    return pl.BlockSpec((n * ROW_TILES, LANES), index_map)


def _store_tile_rows(ref, x, first_row=0):
    n = x.shape[0]
    for s in range(ROW_TILES):
        ref[pl.ds(first_row * ROW_TILES + s, n, stride=ROW_TILES), :] = x[:, s * LANES:(s + 1) * LANES]


def _load_tile_rows(ref, first_row, n):
    return jnp.concatenate([ref[pl.ds(first_row * ROW_TILES + s, n, stride=ROW_TILES), :]
                            for s in range(ROW_TILES)], axis=1)


def _inproj_kernel(x_ref, g_ref, cos_ref, sin_ref, w_ref, wvt_ref,
                   xr_ref, gr_ref, q_ref, k_ref, vt_ref, ga_ref, gb_ref):
    u = _rms(x_ref[...], g_ref[...]).astype(BF16)

    def proj(seg):
        return jnp.dot(u, w_ref[:, seg * D_MODEL:(seg + 1) * D_MODEL], preferred_element_type=F32)

    reps = D_MODEL // LANES
    cos = jnp.concatenate([cos_ref[...]] * reps, axis=1)
    sin = jnp.concatenate([sin_ref[...]] * reps, axis=1)
    lane = lax.broadcasted_iota(jnp.int32, cos.shape, 1)
    first_half = (lane % HEAD_DIM) < (HEAD_DIM // 2)

    def rope(z):
        swapped = jnp.where(first_half,
                            pltpu.roll(z, D_MODEL - HEAD_DIM // 2, 1),
                            pltpu.roll(z, HEAD_DIM // 2, 1))
        return z * cos + swapped * sin

    xr_ref[...] = proj(0).astype(BF16)
    gr_ref[...] = proj(1).astype(BF16)
    q_ref[...] = (rope(proj(2)) * (HEAD_DIM ** -0.5 * LOG2E)).astype(BF16)
    k_ref[...] = rope(proj(3)).astype(BF16)
    ga_ref[...] = proj(4).astype(BF16)
    gb_ref[...] = proj(5).astype(BF16)
    vt = lax.dot_general(wvt_ref[...], u, (((1,), (1,)), ((), ())), preferred_element_type=F32)
    vt_ref[...] = vt.astype(BF16)


def _inproj(x2d, g_mix, cos_t, sin_t, w_rest, w_vt, seq):
    T = x2d.shape[0]
    tm = _tile(seq, 512)
    n_pos = seq // tm
    tok = pl.BlockSpec((tm, D_MODEL), lambda i: (i, 0))
    pos = pl.BlockSpec((tm, LANES), lambda i: (i % n_pos, 0))
    const = lambda shape: pl.BlockSpec(shape, lambda i: (0,) * len(shape), pipeline_mode=pl.Buffered(1))
    outs = [jax.ShapeDtypeStruct((T, D_MODEL), BF16)] * 4
    out_shape = (outs[0], outs[1], outs[2], outs[3],
                 jax.ShapeDtypeStruct((D_MODEL, T), BF16), outs[0], outs[0])
    return pl.pallas_call(
        _inproj_kernel,
        grid=(T // tm,),
        in_specs=[tok, const((1, D_MODEL)), pos, pos,
                  const(w_rest.shape), const(w_vt.shape)],
        out_specs=(tok, tok, tok, tok, pl.BlockSpec((D_MODEL, tm), lambda i: (0, i)), tok, tok),
        out_shape=out_shape,
        compiler_params=_cparams(("parallel",)),
        name="inproj",
    )(x2d, g_mix, cos_t, sin_t, w_rest, w_vt)


def _lru_kernel(xm_ref, xp_ref, xn_ref, cw_ref, cb_ref, wg_ref, bg_ref, lam_ref,
                h_ref, a_s, b_s, carry_s, *, reverse, tm, n_tiles):
    i = pl.program_id(1)
    ti = (n_tiles - 1 - i) if reverse else i
    C = D_MODEL

    @pl.when(i == 0)
    def _():
        carry_s[...] = jnp.zeros_like(carry_s)

    xm = xm_ref[0].astype(F32)
    xp = xp_ref[0].astype(F32) * (ti > 0).astype(F32)
    xn = xn_ref[0].astype(F32) * (ti < n_tiles - 1).astype(F32)
    big = jnp.concatenate([xp, xm, xn], axis=0)
    xc = jnp.broadcast_to(cb_ref[...], (tm, C))
    for j in range(CONV_W):
        off = BF16_ROWS + j - CONV_LEFT
        xc = xc + big[off:off + tm, :] * cw_ref[j:j + 1, :]

    xcb = xc.astype(BF16)
    r_parts, i_parts = [], []
    for g in range(C // LRU_GROUP):
        z = jnp.dot(xcb[:, g * LRU_GROUP:(g + 1) * LRU_GROUP], wg_ref[g], preferred_element_type=F32)
        z = z + bg_ref[g]
        r_parts.append(z[:, :LRU_GROUP])
        i_parts.append(z[:, LRU_GROUP:])
    r = _sigmoid(jnp.concatenate(r_parts, axis=1))
    ig = _sigmoid(jnp.concatenate(i_parts, axis=1))

    neg_lam = -lam_ref[...]
    softplus = jnp.maximum(neg_lam, 0.0) + jnp.log1p(jnp.exp(-jnp.abs(neg_lam)))
    a = jnp.exp((-LRU_C * softplus) * r)
    y = 1.0 - a * a
    a_s[...] = a
    b_s[...] = (y * lax.rsqrt(jnp.maximum(y, 1e-30))) * (ig * xc)

    n_slab = tm // SUBLANES
    row = lax.broadcasted_iota(jnp.int32, (SUBLANES, C), 0)

    def body(s, carry):
        sl = (n_slab - 1 - s) if reverse else s
        r0 = pl.multiple_of(sl * SUBLANES, SUBLANES)
        a8 = a_s[pl.ds(r0, SUBLANES), :]
        b8 = b_s[pl.ds(r0, SUBLANES), :]
        for d in (1, 2, 4):
            if reverse:
                valid = row < SUBLANES - d
                shift = SUBLANES - d
            else:
                valid = row >= d
                shift = d
            a_sh = jnp.where(valid, pltpu.roll(a8, shift, 0), 1.0)
            b_sh = jnp.where(valid, pltpu.roll(b8, shift, 0), 0.0)
            b8 = a8 * b_sh + b8
            a8 = a8 * a_sh
        h8 = b8 + a8 * carry
        b_s[pl.ds(r0, SUBLANES), :] = h8
        edge = h8[0:1, :] if reverse else h8[SUBLANES - 1:SUBLANES, :]
        return jnp.broadcast_to(edge, (SUBLANES, C))

    carry_s[...] = lax.fori_loop(0, n_slab, body, carry_s[...])
    h_ref[0] = b_s[...].astype(BF16)


def _lru(xr3, conv_w, conv_b, wg, bg, lam, reverse):
    B, S, C = xr3.shape
    tm = _tile(S, 256)
    n_tiles = S // tm
    hb = tm // BF16_ROWS
    n_halo = S // BF16_ROWS
    tidx = (lambda i: n_tiles - 1 - i) if reverse else (lambda i: i)
    const = lambda shape: pl.BlockSpec(shape, lambda b, i: (0,) * len(shape))
    kern = functools.partial(_lru_kernel, reverse=reverse, tm=tm, n_tiles=n_tiles)
    return pl.pallas_call(
        kern,
        grid=(B, n_tiles),
        in_specs=[
            pl.BlockSpec((1, tm, C), lambda b, i: (b, tidx(i), 0)),
            pl.BlockSpec((1, BF16_ROWS, C), lambda b, i: (b, jnp.maximum(tidx(i) * hb - 1, 0), 0)),
            pl.BlockSpec((1, BF16_ROWS, C), lambda b, i: (b, jnp.minimum((tidx(i) + 1) * hb, n_halo - 1), 0)),
            const(conv_w.shape), const(conv_b.shape), const(wg.shape), const(bg.shape), const(lam.shape),
        ],
        out_specs=pl.BlockSpec((1, tm, C), lambda b, i: (b, tidx(i), 0)),
        out_shape=jax.ShapeDtypeStruct((B, S, C), BF16),
        scratch_shapes=[pltpu.VMEM((tm, C), F32), pltpu.VMEM((tm, C), F32), pltpu.VMEM((SUBLANES, C), F32)],
        compiler_params=_cparams(("parallel", "arbitrary")),
        name="lru_bwd" if reverse else "lru_fwd",
    )(xr3, xr3, xr3, conv_w, conv_b, wg, bg, lam)


def _attn_kernel(lq1_ref, lk1_ref, lq2_ref, lk2_ref, q_ref, k_ref, vt_ref, g_ref,
                 o_ref, qz_s, s_s, m_s, l_s, acc_s, *, tq, tk, seq, unroll):
    qt = q_ref[0].astype(F32).T
    chan = lax.broadcasted_iota(jnp.int32, qt.shape, 0)
    qz_s[:, 0:tq] = jnp.where(chan < HEAD_DIM, qt, 0.0).astype(BF16)
    qz_s[:, tq:2 * tq] = jnp.where(chan >= HEAD_DIM, qt, 0.0).astype(BF16)
    m_s[...] = jnp.full_like(m_s, -1e30)
    l_s[...] = jnp.zeros_like(l_s)
    acc_s[...] = jnp.zeros_like(acc_s)
    n_chunks = seq // tk

    n_ct = 2 * tq // MXU_N

    def scores(kc, slot, ct):
        cols = slice(ct * MXU_N, (ct + 1) * MXU_N)
        s_s[slot, ct] = jnp.dot(kc, qz_s[:, cols], preferred_element_type=F32)

    def accumulate(vc, slot, ct):
        cols = slice(ct * MXU_N, (ct + 1) * MXU_N)
        s = s_s[slot, ct]
        m_old = m_s[:, cols]
        m_new = jnp.maximum(m_old, jnp.max(s, axis=0, keepdims=True))
        alpha = jnp.exp2(m_old - m_new)
        p = jnp.exp2(s - m_new)
        pv = jnp.dot(vc, p.astype(BF16), preferred_element_type=F32)
        l_s[:, cols] = alpha * l_s[:, cols] + jnp.sum(p, axis=0, keepdims=True)
        acc_s[:, cols] = alpha * acc_s[:, cols] + pv
        m_s[:, cols] = m_new

    def keys(j):
        return k_ref[0, pl.ds(pl.multiple_of(j * tk, tk), tk), :]

    def values(j):
        return vt_ref[:, pl.ds(pl.multiple_of(j * tk, tk), tk)]

    kc = keys(0)
    for ct in range(n_ct):
        scores(kc, 0, ct)

    def body(jj, c):
        for u in range(unroll):
            j = unroll * jj + u
            kc, va = keys(jnp.minimum(j + 1, n_chunks - 1)), values(j)
            for ct in range(n_ct):
                scores(kc, (u + 1) % 2, ct)
                accumulate(va, u % 2, ct)
        return c

    lax.fori_loop(0, n_chunks // unroll, body, 0)

    lam = (jnp.exp(jnp.sum(lq1_ref[...] * lk1_ref[...], axis=1, keepdims=True))
           - jnp.exp(jnp.sum(lq2_ref[...] * lk2_ref[...], axis=1, keepdims=True)) + LAM_INIT)
    o_all = acc_s[...] / l_s[...]
    o = o_all[:, :tq] - lam * o_all[:, tq:]
    y = o * lax.rsqrt(jnp.mean(o * o, axis=0, keepdims=True) + EPS) * g_ref[...] * (1.0 - LAM_INIT)
    o_ref[0] = y.T.astype(BF16)


def _attn(q3, k3, vt, lam_vecs, subln_g):
    B, S, _ = q3.shape
    tq = _tile(S, 1024)
    tk = _tile(S, 512)
    unroll = max(u for u in (2, 4, 8) if (S // tk) % u == 0)
    vec = pl.BlockSpec((1, HEAD_DIM), lambda b, h, i: (0, 0))
    kern = functools.partial(_attn_kernel, tq=tq, tk=tk, seq=S, unroll=unroll)
    return pl.pallas_call(
        kern,
        grid=(B, N_HEADS, S // tq),
        in_specs=[vec, vec, vec, vec,
                  pl.BlockSpec((1, tq, V_DIM), lambda b, h, i: (b, i, h)),
                  pl.BlockSpec((1, S, V_DIM), lambda b, h, i: (b, 0, h)),
                  pl.BlockSpec((V_DIM, S), lambda b, h, i: (h, b)),
                  pl.BlockSpec((V_DIM, 1), lambda b, h, i: (0, 0))],
        out_specs=pl.BlockSpec((1, tq, V_DIM), lambda b, h, i: (b, i, h)),
        out_shape=jax.ShapeDtypeStruct((B, S, N_HEADS * V_DIM), BF16),
        scratch_shapes=[pltpu.VMEM((V_DIM, 2 * tq), BF16), pltpu.VMEM((2, 2 * tq // MXU_N, tk, MXU_N), F32),
                        pltpu.VMEM((1, 2 * tq), F32), pltpu.VMEM((1, 2 * tq), F32),
                        pltpu.VMEM((V_DIM, 2 * tq), F32)],
        compiler_params=_cparams(("parallel", "parallel", "arbitrary")),
        name="attn",
    )(*lam_vecs, q3, k3, vt, subln_g)


def _gelu_tanh(x):
    return 0.5 * x * (1.0 + jnp.tanh(math.sqrt(2.0 / math.pi) * (x + 0.044715 * (x * x * x))))


def _merge_kernel(x_ref, hf_ref, hb_ref, gr_ref, ga_ref, gb_ref, yb_ref, wout_ref, g_ref,
                  wrh_ref, wrl_ref, br_ref, h1_ref, xn_ref, lt_ref):
    f = lambda ref: ref[...].astype(F32)
    ya = _gelu_tanh(f(gr_ref)) * (f(hf_ref) + f(hb_ref))
    mixed = _sigmoid(f(ga_ref)) * ya + _sigmoid(f(gb_ref)) * f(yb_ref)
    h1 = x_ref[...] + jnp.dot(mixed.astype(BF16), wout_ref[...], preferred_element_type=F32)
    h1_ref[...] = h1
    xn = _rms(h1, g_ref[...])
    _store_tile_rows(xn_ref, xn)
    xn_hi = xn.astype(BF16)
    xn_lo = (xn - xn_hi.astype(F32)).astype(BF16)
    nt = lambda w, v: lax.dot_general(w, v, (((1,), (1,)), ((), ())), preferred_element_type=F32)
    lt_ref[...] = nt(wrh_ref[...], xn_hi) + nt(wrh_ref[...], xn_lo) + nt(wrl_ref[...], xn_hi) + br_ref[...]


def _merge(x2d, hf, hb, gr, ga, gb, yb, w_out, g_ffn, wr_hi, wr_lo, b_router):
    T = x2d.shape[0]
    tm = _tile(T, 512)
    tok = pl.BlockSpec((tm, D_MODEL), lambda i: (i, 0))
    const = lambda shape: pl.BlockSpec(shape, lambda i: (0,) * len(shape))
    return pl.pallas_call(
        _merge_kernel,
        grid=(T // tm,),
        in_specs=[tok] * 7 + [const(w_out.shape), const(g_ffn.shape), const(wr_hi.shape),
                              const(wr_lo.shape), const(b_router.shape)],
        out_specs=(tok, _tile_rows_spec(tm, lambda i: (i, 0)), pl.BlockSpec((N_EXPERTS, tm), lambda i: (0, i))),
        out_shape=(jax.ShapeDtypeStruct((T, D_MODEL), F32), _tile_rows_shape(T),
                   jax.ShapeDtypeStruct((N_EXPERTS, T), F32)),
        compiler_params=_cparams(("parallel",)),
        name="merge",
    )(x2d, hf, hb, gr, ga, gb, yb, w_out, g_ffn, wr_hi, wr_lo, b_router)


def _route_kernel(lt_ref, tri_ref, idx_ref, gate_ref, rank_ref, cnt_ref, carry_s, *, tn):
    i = pl.program_id(0)

    @pl.when(i == 0)
    def _():
        carry_s[...] = jnp.zeros_like(carry_s)

    work = lt_ref[...]
    eidx = lax.broadcasted_iota(jnp.int32, work.shape, 0)
    vals, idxs = [], []
    for _ in range(TOP_K):
        mx = jnp.max(work, axis=0, keepdims=True)
        ix = jnp.min(jnp.where(work == mx, eidx, N_EXPERTS), axis=0, keepdims=True)
        vals.append(mx)
        idxs.append(ix)
        work = jnp.where(eidx == ix, -jnp.inf, work)
    ex = [jnp.exp(v - vals[0]) for v in vals]
    denom = ex[0] + ex[1] + ex[2] + ex[3]
    sel = jnp.zeros(work.shape, F32)
    for ix in idxs:
        sel = sel + (eidx == ix).astype(F32)
    before = jnp.dot(sel.astype(BF16), tri_ref[...], preferred_element_type=F32) + carry_s[:, 0:1]
    for k in range(TOP_K):
        rank = jnp.sum(jnp.where(eidx == idxs[k], before, 0.0), axis=0, keepdims=True)
        rank_ref[k:k + 1, :] = rank.astype(jnp.int32)
        idx_ref[k:k + 1, :] = idxs[k]
        gate_ref[k:k + 1, :] = ex[k] / denom
    carry_s[...] = carry_s[...] + jnp.sum(sel, axis=1, keepdims=True)
    cnt_ref[...] = carry_s[...].astype(jnp.int32)


def _route(logits_t):
    E, T = logits_t.shape
    tn = _tile(T, 1024)
    tri = (lax.broadcasted_iota(jnp.int32, (tn, tn), 0) < lax.broadcasted_iota(jnp.int32, (tn, tn), 1)).astype(BF16)
    kt = pl.BlockSpec((TOP_K, tn), lambda i: (0, i))
    kern = functools.partial(_route_kernel, tn=tn)
    return pl.pallas_call(
        kern,
        grid=(T // tn,),
        in_specs=[pl.BlockSpec((E, tn), lambda i: (0, i)), pl.BlockSpec((tn, tn), lambda i: (0, 0))],
        out_specs=(kt, kt, kt, pl.BlockSpec((E, LANES), lambda i: (0, 0))),
        out_shape=(jax.ShapeDtypeStruct((TOP_K, T), jnp.int32), jax.ShapeDtypeStruct((TOP_K, T), F32),
                   jax.ShapeDtypeStruct((TOP_K, T), jnp.int32), jax.ShapeDtypeStruct((E, LANES), jnp.int32)),
        scratch_shapes=[pltpu.VMEM((E, LANES), F32)],
        compiler_params=_cparams(("arbitrary",)),
        name="route",
    )(logits_t, tri)


def _row_copy(src, src_row, dst, dst_row, sem):
    tile = lambda r: pl.ds(pl.multiple_of(r * ROW_TILES, ROW_TILES), ROW_TILES)
    return pltpu.make_async_copy(src.at[tile(src_row)], dst.at[tile(dst_row)], sem)


def _wait_rows(src, dst, n_rows, sem):
    n = pl.multiple_of(n_rows * ROW_TILES, ROW_TILES)
    pltpu.make_async_copy(src.at[pl.ds(0, n)], dst.at[pl.ds(0, n)], sem).wait()


def _scatter_kernel(fill_lo_ref, fill_hi_ref, dest_ref, xn_ref, xs_ref, zero_s, sem, zsem, *, tt):
    i = pl.program_id(0)

    @pl.when(i == 0)
    def _():
        zero_s[...] = jnp.zeros_like(zero_s)

        def per_expert(e, n):
            lo, hi = fill_lo_ref[e], fill_hi_ref[e]

            def fill(r, c):
                _row_copy(zero_s, 0, xs_ref, r, zsem).start()
                return c

            lax.fori_loop(lo, hi, fill, 0)
            return n + (hi - lo)

        n_fill = lax.fori_loop(0, N_EXPERTS, per_expert, 0)

        @pl.when(n_fill > 0)
        def _():
            _wait_rows(xs_ref, xs_ref, n_fill, zsem)

    def body(t, c):
        for k in range(TOP_K):
            _row_copy(xn_ref, t, xs_ref, dest_ref[t * TOP_K + k], sem).start()
        return c

    lax.fori_loop(0, tt, body, 0, unroll=2)
    _wait_rows(xs_ref, xs_ref, TOP_K * tt, sem)


def _scatter_rows(fill_lo, fill_hi, dest_flat, xn, n_rows):
    T = xn.shape[0] // ROW_TILES
    tt = _tile(T, 512)
    kern = functools.partial(_scatter_kernel, tt=tt)
    return pl.pallas_call(
        kern,
        grid_spec=pltpu.PrefetchScalarGridSpec(
            num_scalar_prefetch=2,
            grid=(T // tt,),
            in_specs=[pl.BlockSpec((TOP_K * tt,), lambda i, lo, hi: (i,), memory_space=pltpu.SMEM),
                      _tile_rows_spec(tt, lambda i, lo, hi: (i, 0))],
            out_specs=pl.BlockSpec(memory_space=pl.ANY),
            scratch_shapes=[pltpu.VMEM((ROW_TILES, LANES), F32), pltpu.SemaphoreType.DMA(()),
                            pltpu.SemaphoreType.DMA(())],
        ),
        out_shape=_tile_rows_shape(n_rows),
        compiler_params=_cparams(("arbitrary",)),
        name="scatter",
    )(fill_lo, fill_hi, dest_flat, xn)


def _split_w1_kernel(w_ref, perm_ref, wg_ref, wl_ref):
    z = jnp.dot(w_ref[0].astype(BF16), perm_ref[...], preferred_element_type=F32)
    wg_ref[0] = z[:, :D_MODEL].astype(BF16)
    wl_ref[0] = z[:, D_MODEL:].astype(BF16)


def _split_w1(w1):
    E, D, D2 = w1.shape
    tr = _tile(D, 512)
    col = lax.broadcasted_iota(jnp.int32, (D2, D2), 1)
    row = lax.broadcasted_iota(jnp.int32, (D2, D2), 0)
    perm = (row == jnp.where(col < D2 // 2, 2 * col, 2 * (col - D2 // 2) + 1)).astype(BF16)
    out = pl.BlockSpec((1, tr, D2 // 2), lambda e, i: (e, i, 0))
    return pl.pallas_call(
        _split_w1_kernel,
        grid=(E, D // tr),
        in_specs=[pl.BlockSpec((1, tr, D2), lambda e, i: (e, i, 0)),
                  pl.BlockSpec((D2, D2), lambda e, i: (0, 0), pipeline_mode=pl.Buffered(1))],
        out_specs=(out, out),
        out_shape=(jax.ShapeDtypeStruct((E, D, D2 // 2), BF16),) * 2,
        compiler_params=_cparams(("parallel", "parallel")),
        name="split_w1",
    )(w1, perm)


def _ffn_kernel(be_ref, nu_ref, xs_ref, w1g_ref, w1l_ref, b1g_ref, b1l_ref, w2_ref, b2_ref, ys_ref):
    i = pl.program_id(0)

    @pl.when(i < nu_ref[0])
    def _():
        x = _load_tile_rows(xs_ref, 0, ROWS_PER_BLOCK).astype(BF16)
        hg = jnp.dot(x, w1g_ref[0], preferred_element_type=F32) + b1g_ref[0]
        hl = jnp.dot(x, w1l_ref[0], preferred_element_type=F32) + b1l_ref[0]
        xg = jnp.minimum(hg, SWIGLU_LIMIT)
        xl = jnp.clip(hl, -SWIGLU_LIMIT, SWIGLU_LIMIT)
        act = xg * _sigmoid(SWIGLU_ALPHA * xg) * (xl + 1.0)
        _store_tile_rows(ys_ref, jnp.dot(act.astype(BF16), w2_ref[0], preferred_element_type=F32) + b2_ref[0])

    @pl.when(i >= nu_ref[0])
    def _():
        ys_ref[...] = jnp.zeros_like(ys_ref)


def _ffn(block_e, n_used, xs, w1g, w1l, b1g, b1l, w2, b2):
    n_rows = xs.shape[0] // ROW_TILES
    n_blocks = n_rows // ROWS_PER_BLOCK
    rows = _tile_rows_spec(ROWS_PER_BLOCK, lambda i, be, nu: (i, 0))
    rows_in = _tile_rows_spec(ROWS_PER_BLOCK, lambda i, be, nu: (jnp.minimum(i, nu[0] - 1), 0))
    wspec = pl.BlockSpec((1, D_MODEL, D_MODEL), lambda i, be, nu: (be[i], 0, 0))
    bspec = pl.BlockSpec((1, 1, D_MODEL), lambda i, be, nu: (be[i], 0, 0))
    return pl.pallas_call(
        _ffn_kernel,
        grid_spec=pltpu.PrefetchScalarGridSpec(
            num_scalar_prefetch=2,
            grid=(n_blocks,),
            in_specs=[rows_in, wspec, wspec, bspec, bspec, wspec, bspec],
            out_specs=rows,
        ),
        out_shape=_tile_rows_shape(n_rows),
        compiler_params=_cparams(("arbitrary",)),
        name="ffn",
    )(block_e, n_used, xs, w1g, w1l, b1g, b1l, w2, b2)


def _final_kernel(dest_ref, h1_ref, ys_ref, gate_ref, p_ref, gpl_ref, wpg_ref, wpe_ref, gfin_ref,
                  out_ref, ybuf, sems, *, tm):
    half = tm // 2

    def issue(h):
        def body(t, c):
            tt = h * half + t
            for k in range(TOP_K):
                _row_copy(ys_ref, dest_ref[tt * TOP_K + k], ybuf, k * tm + tt, sems.at[h]).start()
            return c
        lax.fori_loop(0, half, body, 0, unroll=2)

    def finish(h):
        _wait_rows(ys_ref, ybuf, TOP_K * half, sems.at[h])
        rows = pl.ds(h * half, half)
        h2 = h1_ref[rows, :]
        for k in range(TOP_K):
            h2 = h2 + gate_ref[rows, k:k + 1] * _load_tile_rows(ybuf, k * tm + h * half, half)
        hn = _rms(h2, gpl_ref[...]).astype(BF16)
        gate = _sigmoid(jnp.dot(hn, wpg_ref[...], preferred_element_type=F32))
        pe = jnp.dot(p_ref[rows, :].astype(BF16), wpe_ref[...], preferred_element_type=F32)
        out_ref[rows, :] = _rms(h2 + gate * pe, gfin_ref[...])

    issue(0)
    issue(1)
    finish(0)
    finish(1)


def _final(dest_flat, h1, ys, gates_tk, p2d, g_pl, w_pg, w_pe, g_final):
    T = h1.shape[0]
    tm = _tile(T, 256)
    tok = pl.BlockSpec((tm, D_MODEL), lambda i: (i, 0))
    const = lambda shape: pl.BlockSpec(shape, lambda i: (0,) * len(shape))
    kern = functools.partial(_final_kernel, tm=tm)
    return pl.pallas_call(
        kern,
        grid=(T // tm,),
        in_specs=[pl.BlockSpec((TOP_K * tm,), lambda i: (i,), memory_space=pltpu.SMEM),
                  tok, pl.BlockSpec(memory_space=pl.ANY),
                  pl.BlockSpec((tm, TOP_K), lambda i: (i, 0)), pl.BlockSpec((tm, PLE_DIM), lambda i: (i, 0)),
                  const(g_pl.shape), const(w_pg.shape), const(w_pe.shape), const(g_final.shape)],
        out_specs=tok,
        out_shape=jax.ShapeDtypeStruct((T, D_MODEL), F32),
        scratch_shapes=[pltpu.VMEM((TOP_K * tm * ROW_TILES, LANES), F32), pltpu.SemaphoreType.DMA((2,))],
        compiler_params=_cparams(("arbitrary",)),
        name="final",
    )(dest_flat, h1, ys, gates_tk, p2d, g_pl, w_pg, w_pe, g_final)


def _rope_tables(S):
    pos = jnp.arange(S, dtype=F32)
    inv = ROPE_THETA ** (-jnp.arange(0, HEAD_DIM, 2, dtype=F32) / HEAD_DIM)
    ang = pos[:, None] * inv[None, :]
    ang = jnp.concatenate([ang, ang, ang, ang], axis=-1)
    sign = jnp.where((jnp.arange(LANES) % HEAD_DIM) < HEAD_DIM // 2, -1.0, 1.0).astype(F32)
    return jnp.cos(ang), jnp.sin(ang) * sign


def _block_diag_groups(w):
    per = LRU_GROUP // LRU_BLOCK
    g = w.shape[0] // per
    eye = jnp.eye(per, dtype=w.dtype)
    return jnp.einsum('gbij,bc->gbicj', w.reshape(g, per, LRU_BLOCK, LRU_BLOCK), eye).reshape(g, LRU_GROUP, LRU_GROUP)


def _prep_weights(g_mix, w_in, conv_w, conv_b, lru_wa, lru_ba, lru_wx, lru_bx, lru_lambda,
                  lam_q1, lam_k1, lam_q2, lam_k2, subln_g, w_out, g_ffn, w_router, b_router,
                  w1, b1, w2, b2, g_pl, w_pg, w_pe, g_final):
    l = 0
    W = D_MODEL
    w = w_in[l]
    pw = dict(
        g_mix=g_mix[l][None, :],
        w_rest=jnp.concatenate([w[:, :4 * W], w[:, 5 * W:]], axis=1).astype(BF16),
        w_vt=w[:, 4 * W:5 * W].T.astype(BF16),
        conv_w=conv_w[l], conv_b=conv_b[l][None, :],
        lam_vecs=tuple(v[l][None, :] for v in (lam_q1, lam_k1, lam_q2, lam_k2)),
        subln_g=subln_g[l][:, None],
        w_out=w_out[l].astype(BF16), g_ffn=g_ffn[l][None, :],
        b_router=b_router[l][:, None],
        b1g=b1[l][:, None, 0::2], b1l=b1[l][:, None, 1::2],
        w2=w2[l].astype(BF16), b2=b2[l][:, None, :],
        g_pl=g_pl[l][None, :], w_pg=w_pg[l].astype(BF16), w_pe=w_pe[l].astype(BF16),
        g_final=g_final[None, :],
    )
    pw['w1g'], pw['w1l'] = _split_w1(w1[l])
    wr_t = w_router[l].T
    wr_hi = wr_t.astype(BF16)
    pw['wr_hi'] = wr_hi
    pw['wr_lo'] = (wr_t - wr_hi.astype(F32)).astype(BF16)
    for d in range(2):
        wa = _block_diag_groups(lru_wa[l, d])
        wx = _block_diag_groups(lru_wx[l, d])
        pw[f'lru_wg{d}'] = jnp.concatenate([wa, wx], axis=2).astype(BF16)
        ng = D_MODEL // LRU_GROUP
        pw[f'lru_bg{d}'] = jnp.concatenate([lru_ba[l, d].reshape(ng, 1, LRU_GROUP),
                                            lru_bx[l, d].reshape(ng, 1, LRU_GROUP)], axis=2)
        pw[f'lru_lam{d}'] = lru_lambda[l, d][None, :]
    return pw


def _moe_plan(idx, rank, counts, T):
    padded = ((counts + ROWS_PER_BLOCK - 1) // ROWS_PER_BLOCK) * ROWS_PER_BLOCK
    pad_end = jnp.cumsum(padded)
    pad_start = pad_end - padded
    onehot = idx[..., None] == jnp.arange(N_EXPERTS, dtype=idx.dtype)
    dest = jnp.sum(jnp.where(onehot, pad_start.astype(idx.dtype), 0), axis=-1) + rank
    n_blocks = -(-(T * TOP_K) // ROWS_PER_BLOCK) + N_EXPERTS
    block_start = jnp.arange(n_blocks, dtype=pad_end.dtype) * ROWS_PER_BLOCK
    block_e = jnp.minimum(jnp.sum(pad_end[None, :] <= block_start[:, None], axis=1),
                          N_EXPERTS - 1).astype(jnp.int32)
    n_used = (pad_end[-1:] // ROWS_PER_BLOCK).astype(jnp.int32)
    dest_flat = dest.T.reshape(-1)
    n_rows = n_blocks * ROWS_PER_BLOCK
    fill_lo = (pad_start + counts).astype(jnp.int32)
    fill_hi = pad_end.astype(jnp.int32).at[N_EXPERTS - 1].set(n_rows)
    return dest_flat, fill_lo, fill_hi, block_e, n_used, n_rows


def _trunk(x, p, pw):
    B, S, _ = x.shape
    T = B * S
    x2d = x.reshape(T, D_MODEL)
    cos_t, sin_t = _rope_tables(S)
    xr, gr, q, k, vt, ga, gb = _inproj(x2d, pw['g_mix'], cos_t, sin_t, pw['w_rest'], pw['w_vt'], S)
    xr3 = xr.reshape(B, S, D_MODEL)
    hf = _lru(xr3, pw['conv_w'], pw['conv_b'], pw['lru_wg0'], pw['lru_bg0'], pw['lru_lam0'], False)
    hb = _lru(xr3, pw['conv_w'], pw['conv_b'], pw['lru_wg1'], pw['lru_bg1'], pw['lru_lam1'], True)
    yb = _attn(q.reshape(B, S, D_MODEL), k.reshape(B, S, D_MODEL), vt, pw['lam_vecs'], pw['subln_g'])
    h1, xn, logits_t = _merge(x2d, hf.reshape(T, D_MODEL), hb.reshape(T, D_MODEL), gr, ga, gb,
                              yb.reshape(T, D_MODEL), pw['w_out'], pw['g_ffn'],
                              pw['wr_hi'], pw['wr_lo'], pw['b_router'])
    idx, gates, rank, cnt = _route(logits_t)
    dest_flat, fill_lo, fill_hi, block_e, n_used, n_rows = _moe_plan(idx, rank, cnt[:, 0], T)
    xs = _scatter_rows(fill_lo, fill_hi, dest_flat, xn, n_rows)
    ys = _ffn(block_e, n_used, xs, pw['w1g'], pw['w1l'], pw['b1g'], pw['b1l'], pw['w2'], pw['b2'])
    out = _final(dest_flat, h1, ys, gates.T, p[0].reshape(T, PLE_DIM),
                 pw['g_pl'], pw['w_pg'], pw['w_pe'], pw['g_final'])
    return out.reshape(B, S, D_MODEL)


def kernel(x_prompt, x_sample, p_prompt, p_sample, g_mix, w_in, conv_w, conv_b, lru_wa, lru_ba, lru_wx, lru_bx, lru_lambda, lam_q1, lam_k1, lam_q2, lam_k2, subln_g, w_out, g_ffn, w_router, b_router, w1, b1, w2, b2, g_pl, w_pg, w_pe, g_final):
    pw = _prep_weights(g_mix, w_in, conv_w, conv_b, lru_wa, lru_ba, lru_wx, lru_bx, lru_lambda,
                       lam_q1, lam_k1, lam_q2, lam_k2, subln_g, w_out, g_ffn, w_router, b_router,
                       w1, b1, w2, b2, g_pl, w_pg, w_pe, g_final)
    return (_trunk(x_prompt, p_prompt, pw), _trunk(x_sample, p_sample, pw))
```

```python
import functools
import math

import jax
import jax.numpy as jnp
from jax import lax
from jax.experimental import pallas as pl
from jax.experimental.pallas import tpu as pltpu

F32 = jnp.float32
BF16 = jnp.bfloat16

D_MODEL = 1024
LRU_BLOCK = 64
LRU_C = 8.0
CONV_W = 4
CONV_LEFT = 2
N_HEADS = 8
HEAD_DIM = 64
V_DIM = 128
ROPE_THETA = 10000.0
N_EXPERTS = 32
TOP_K = 4
SWIGLU_LIMIT = 7.0
SWIGLU_ALPHA = 1.702
ROWS_PER_BLOCK = 512
PLE_DIM = 256
EPS = 1e-6
LAM_INIT = 0.8 - 0.6 * math.exp(-0.3 * 0)
LOG2E = 1.4426950408889634

LANES = 128
SUBLANES = 8
BF16_ROWS = 16
MXU_N = 256
LRU_GROUP = MXU_N
VMEM_LIMIT = 56 * 1024 * 1024


def _tile(n, pref):
    t = min(n, pref)
    while n % t:
        t //= 2
    return t


def _cparams(sem):
    return pltpu.CompilerParams(dimension_semantics=sem, vmem_limit_bytes=VMEM_LIMIT)


def _rms(x, g):
    return x * lax.rsqrt(jnp.mean(x * x, axis=-1, keepdims=True) + EPS) * g


def _sigmoid(x):
    return 1.0 / (1.0 + jnp.exp(-x))


ROW_TILES = D_MODEL // LANES


def _tile_rows_shape(n):
    return jax.ShapeDtypeStruct((n * ROW_TILES, LANES), F32)


def _tile_rows_spec(n, index_map):
    return pl.BlockSpec((n * ROW_TILES, LANES), index_map)


def _store_tile_rows(ref, x, first_row=0):
    n = x.shape[0]
    for s in range(ROW_TILES):
        ref[pl.ds(first_row * ROW_TILES + s, n, stride=ROW_TILES), :] = x[:, s * LANES:(s + 1) * LANES]


def _load_tile_rows(ref, first_row, n):
    return jnp.concatenate([ref[pl.ds(first_row * ROW_TILES + s, n, stride=ROW_TILES), :]
                            for s in range(ROW_TILES)], axis=1)


def _inproj_kernel(x_ref, g_ref, cos_ref, sin_ref, w_ref, wvt_ref,
                   xr_ref, gr_ref, q_ref, k_ref, vt_ref, ga_ref, gb_ref):
    u = _rms(x_ref[...], g_ref[...]).astype(BF16)

    def proj(seg):
        return jnp.dot(u, w_ref[:, seg * D_MODEL:(seg + 1) * D_MODEL], preferred_element_type=F32)

    reps = D_MODEL // LANES
    cos = jnp.concatenate([cos_ref[...]] * reps, axis=1)
    sin = jnp.concatenate([sin_ref[...]] * reps, axis=1)
    lane = lax.broadcasted_iota(jnp.int32, cos.shape, 1)
    first_half = (lane % HEAD_DIM) < (HEAD_DIM // 2)

    def rope(z):
        swapped = jnp.where(first_half,
                            pltpu.roll(z, D_MODEL - HEAD_DIM // 2, 1),
                            pltpu.roll(z, HEAD_DIM // 2, 1))
        return z * cos + swapped * sin

    xr_ref[...] = proj(0).astype(BF16)
    gr_ref[...] = proj(1).astype(BF16)
    q_ref[...] = (rope(proj(2)) * (HEAD_DIM ** -0.5 * LOG2E)).astype(BF16)
    k_ref[...] = rope(proj(3)).astype(BF16)
    ga_ref[...] = proj(4).astype(BF16)
    gb_ref[...] = proj(5).astype(BF16)
    vt = lax.dot_general(wvt_ref[...], u, (((1,), (1,)), ((), ())), preferred_element_type=F32)
    vt_ref[...] = vt.astype(BF16)


def _inproj(x2d, g_mix, cos_t, sin_t, w_rest, w_vt, seq):
    T = x2d.shape[0]
    tm = _tile(seq, 512)
    n_pos = seq // tm
    tok = pl.BlockSpec((tm, D_MODEL), lambda i: (i, 0))
    pos = pl.BlockSpec((tm, LANES), lambda i: (i % n_pos, 0))
    const = lambda shape: pl.BlockSpec(shape, lambda i: (0,) * len(shape), pipeline_mode=pl.Buffered(1))
    outs = [jax.ShapeDtypeStruct((T, D_MODEL), BF16)] * 4
    out_shape = (outs[0], outs[1], outs[2], outs[3],
                 jax.ShapeDtypeStruct((D_MODEL, T), BF16), outs[0], outs[0])
    return pl.pallas_call(
        _inproj_kernel,
        grid=(T // tm,),
        in_specs=[tok, const((1, D_MODEL)), pos, pos,
                  const(w_rest.shape), const(w_vt.shape)],
        out_specs=(tok, tok, tok, tok, pl.BlockSpec((D_MODEL, tm), lambda i: (0, i)), tok, tok),
        out_shape=out_shape,
        compiler_params=_cparams(("parallel",)),
        name="inproj",
    )(x2d, g_mix, cos_t, sin_t, w_rest, w_vt)


def _lru_kernel(xm_ref, xp_ref, xn_ref, cw_ref, cb_ref, wg_ref, bg_ref, lam_ref,
                h_ref, a_s, b_s, carry_s, *, reverse, tm, n_tiles):
    i = pl.program_id(1)
    ti = (n_tiles - 1 - i) if reverse else i
    C = D_MODEL

    @pl.when(i == 0)
    def _():
        carry_s[...] = jnp.zeros_like(carry_s)

    xm = xm_ref[0].astype(F32)
    xp = xp_ref[0].astype(F32) * (ti > 0).astype(F32)
    xn = xn_ref[0].astype(F32) * (ti < n_tiles - 1).astype(F32)
    big = jnp.concatenate([xp, xm, xn], axis=0)
    xc = jnp.broadcast_to(cb_ref[...], (tm, C))
    n_big = tm + 2 * BF16_ROWS
    for j in range(CONV_W):
        shift = (CONV_LEFT - j) % n_big
        shifted = big if shift == 0 else pltpu.roll(big, shift, 0)
        xc = xc + shifted[BF16_ROWS:BF16_ROWS + tm, :] * cw_ref[j:j + 1, :]

    xcb = xc.astype(BF16)
    r_parts, i_parts = [], []
    for g in range(C // LRU_GROUP):
        z = jnp.dot(xcb[:, g * LRU_GROUP:(g + 1) * LRU_GROUP], wg_ref[g], preferred_element_type=F32)
        z = z + bg_ref[g]
        r_parts.append(z[:, :LRU_GROUP])
        i_parts.append(z[:, LRU_GROUP:])
    r = _sigmoid(jnp.concatenate(r_parts, axis=1))
    ig = _sigmoid(jnp.concatenate(i_parts, axis=1))

    neg_lam = -lam_ref[...]
    softplus = jnp.maximum(neg_lam, 0.0) + jnp.log1p(jnp.exp(-jnp.abs(neg_lam)))
    a = jnp.exp((-LRU_C * softplus) * r)
    y = 1.0 - a * a
    a_s[...] = a
    b_s[...] = (y * lax.rsqrt(jnp.maximum(y, 1e-30))) * (ig * xc)

    n_slab = tm // SUBLANES
    row = lax.broadcasted_iota(jnp.int32, (SUBLANES, C), 0)

    def body(s, carry):
        sl = (n_slab - 1 - s) if reverse else s
        r0 = pl.multiple_of(sl * SUBLANES, SUBLANES)
        a8 = a_s[pl.ds(r0, SUBLANES), :]
        b8 = b_s[pl.ds(r0, SUBLANES), :]
        for d in (1, 2, 4):
            if reverse:
                valid = row < SUBLANES - d
                shift = SUBLANES - d
            else:
                valid = row >= d
                shift = d
            a_sh = jnp.where(valid, pltpu.roll(a8, shift, 0), 1.0)
            b_sh = jnp.where(valid, pltpu.roll(b8, shift, 0), 0.0)
            b8 = a8 * b_sh + b8
            a8 = a8 * a_sh
        h8 = b8 + a8 * carry
        b_s[pl.ds(r0, SUBLANES), :] = h8
        edge = h8[0:1, :] if reverse else h8[SUBLANES - 1:SUBLANES, :]
        return jnp.broadcast_to(edge, (SUBLANES, C))

    carry_s[...] = lax.fori_loop(0, n_slab, body, carry_s[...], unroll=4)
    h_ref[0] = b_s[...].astype(BF16)


def _lru(xr3, conv_w, conv_b, wg, bg, lam, reverse):
    B, S, C = xr3.shape
    tm = _tile(S, 256)
    n_tiles = S // tm
    hb = tm // BF16_ROWS
    n_halo = S // BF16_ROWS
    tidx = (lambda i: n_tiles - 1 - i) if reverse else (lambda i: i)
    const = lambda shape: pl.BlockSpec(shape, lambda b, i: (0,) * len(shape))
    kern = functools.partial(_lru_kernel, reverse=reverse, tm=tm, n_tiles=n_tiles)
    return pl.pallas_call(
        kern,
        grid=(B, n_tiles),
        in_specs=[
            pl.BlockSpec((1, tm, C), lambda b, i: (b, tidx(i), 0)),
            pl.BlockSpec((1, BF16_ROWS, C), lambda b, i: (b, jnp.maximum(tidx(i) * hb - 1, 0), 0)),
            pl.BlockSpec((1, BF16_ROWS, C), lambda b, i: (b, jnp.minimum((tidx(i) + 1) * hb, n_halo - 1), 0)),
            const(conv_w.shape), const(conv_b.shape), const(wg.shape), const(bg.shape), const(lam.shape),
        ],
        out_specs=pl.BlockSpec((1, tm, C), lambda b, i: (b, tidx(i), 0)),
        out_shape=jax.ShapeDtypeStruct((B, S, C), BF16),
        scratch_shapes=[pltpu.VMEM((tm, C), F32), pltpu.VMEM((tm, C), F32), pltpu.VMEM((SUBLANES, C), F32)],
        compiler_params=_cparams(("parallel", "arbitrary")),
        name="lru_bwd" if reverse else "lru_fwd",
    )(xr3, xr3, xr3, conv_w, conv_b, wg, bg, lam)


def _attn_kernel(lq1_ref, lk1_ref, lq2_ref, lk2_ref, q_ref, k_ref, vt_ref, g_ref,
                 o_ref, qz_s, s_s, m_s, l_s, acc_s, *, tq, tk, seq, unroll):
    qt = q_ref[0].astype(F32).T
    chan = lax.broadcasted_iota(jnp.int32, qt.shape, 0)
    qz_s[:, 0:tq] = jnp.where(chan < HEAD_DIM, qt, 0.0).astype(BF16)
    qz_s[:, tq:2 * tq] = jnp.where(chan >= HEAD_DIM, qt, 0.0).astype(BF16)
    m_s[...] = jnp.full_like(m_s, -1e30)
    l_s[...] = jnp.zeros_like(l_s)
    acc_s[...] = jnp.zeros_like(acc_s)
    n_chunks = seq // tk

    n_ct = 2 * tq // MXU_N

    def scores(kc, slot, ct):
        cols = slice(ct * MXU_N, (ct + 1) * MXU_N)
        s_s[slot, ct] = jnp.dot(kc, qz_s[:, cols], preferred_element_type=F32)

    def accumulate(vc, slot, ct):
        cols = slice(ct * MXU_N, (ct + 1) * MXU_N)
        s = s_s[slot, ct]
        m_old = m_s[:, cols]
        m_new = jnp.maximum(m_old, jnp.max(s, axis=0, keepdims=True))
        alpha = jnp.exp2(m_old - m_new)
        p = jnp.exp2(s - m_new)
        pv = jnp.dot(vc, p.astype(BF16), preferred_element_type=F32)
        l_s[:, cols] = alpha * l_s[:, cols] + jnp.sum(p, axis=0, keepdims=True)
        acc_s[:, cols] = alpha * acc_s[:, cols] + pv
        m_s[:, cols] = m_new

    def keys(j):
        return k_ref[0, pl.ds(pl.multiple_of(j * tk, tk), tk), :]

    def values(j):
        return vt_ref[:, pl.ds(pl.multiple_of(j * tk, tk), tk)]

    kc = keys(0)
    for ct in range(n_ct):
        scores(kc, 0, ct)

    def body(jj, c):
        for u in range(unroll):
            j = unroll * jj + u
            kc, va = keys(jnp.minimum(j + 1, n_chunks - 1)), values(j)
            for ct in range(n_ct):
                scores(kc, (u + 1) % 2, ct)
                accumulate(va, u % 2, ct)
        return c

    lax.fori_loop(0, n_chunks // unroll, body, 0)

    lam = (jnp.exp(jnp.sum(lq1_ref[...] * lk1_ref[...], axis=1, keepdims=True))
           - jnp.exp(jnp.sum(lq2_ref[...] * lk2_ref[...], axis=1, keepdims=True)) + LAM_INIT)
    o_all = acc_s[...] / l_s[...]
    o = o_all[:, :tq] - lam * o_all[:, tq:]
    y = o * lax.rsqrt(jnp.mean(o * o, axis=0, keepdims=True) + EPS) * g_ref[...] * (1.0 - LAM_INIT)
    o_ref[0] = y.T.astype(BF16)


def _attn(q3, k3, vt, lam_vecs, subln_g):
    B, S, _ = q3.shape
    tq = _tile(S, 1024)
    tk = _tile(S, 512)
    unroll = max(u for u in (2, 4, 8) if (S // tk) % u == 0)
    vec = pl.BlockSpec((1, HEAD_DIM), lambda b, h, i: (0, 0))
    kern = functools.partial(_attn_kernel, tq=tq, tk=tk, seq=S, unroll=unroll)
    return pl.pallas_call(
        kern,
        grid=(B, N_HEADS, S // tq),
        in_specs=[vec, vec, vec, vec,
                  pl.BlockSpec((1, tq, V_DIM), lambda b, h, i: (b, i, h)),
                  pl.BlockSpec((1, S, V_DIM), lambda b, h, i: (b, 0, h)),
                  pl.BlockSpec((V_DIM, S), lambda b, h, i: (h, b)),
                  pl.BlockSpec((V_DIM, 1), lambda b, h, i: (0, 0))],
        out_specs=pl.BlockSpec((1, tq, V_DIM), lambda b, h, i: (b, i, h)),
        out_shape=jax.ShapeDtypeStruct((B, S, N_HEADS * V_DIM), BF16),
        scratch_shapes=[pltpu.VMEM((V_DIM, 2 * tq), BF16), pltpu.VMEM((2, 2 * tq // MXU_N, tk, MXU_N), F32),
                        pltpu.VMEM((1, 2 * tq), F32), pltpu.VMEM((1, 2 * tq), F32),
                        pltpu.VMEM((V_DIM, 2 * tq), F32)],
        compiler_params=_cparams(("parallel", "parallel", "arbitrary")),
        name="attn",
    )(*lam_vecs, q3, k3, vt, subln_g)


def _gelu_tanh(x):
    return 0.5 * x * (1.0 + jnp.tanh(math.sqrt(2.0 / math.pi) * (x + 0.044715 * (x * x * x))))


def _merge_kernel(x_ref, hf_ref, hb_ref, gr_ref, ga_ref, gb_ref, yb_ref, wout_ref, g_ref,
                  wrh_ref, wrl_ref, br_ref, h1_ref, xn_ref, lt_ref):
    f = lambda ref: ref[...].astype(F32)
    ya = _gelu_tanh(f(gr_ref)) * (f(hf_ref) + f(hb_ref))
    mixed = _sigmoid(f(ga_ref)) * ya + _sigmoid(f(gb_ref)) * f(yb_ref)
    h1 = x_ref[...] + jnp.dot(mixed.astype(BF16), wout_ref[...], preferred_element_type=F32)
    h1_ref[...] = h1
    xn = _rms(h1, g_ref[...])
    _store_tile_rows(xn_ref, xn)
    xn_hi = xn.astype(BF16)
    xn_lo = (xn - xn_hi.astype(F32)).astype(BF16)
    nt = lambda w, v: lax.dot_general(w, v, (((1,), (1,)), ((), ())), preferred_element_type=F32)
    lt_ref[...] = nt(wrh_ref[...], xn_hi) + nt(wrh_ref[...], xn_lo) + nt(wrl_ref[...], xn_hi) + br_ref[...]


def _merge(x2d, hf, hb, gr, ga, gb, yb, w_out, g_ffn, wr_hi, wr_lo, b_router):
    T = x2d.shape[0]
    tm = _tile(T, 512)
    tok = pl.BlockSpec((tm, D_MODEL), lambda i: (i, 0))
    const = lambda shape: pl.BlockSpec(shape, lambda i: (0,) * len(shape))
    return pl.pallas_call(
        _merge_kernel,
        grid=(T // tm,),
        in_specs=[tok] * 7 + [const(w_out.shape), const(g_ffn.shape), const(wr_hi.shape),
                              const(wr_lo.shape), const(b_router.shape)],
        out_specs=(tok, _tile_rows_spec(tm, lambda i: (i, 0)), pl.BlockSpec((N_EXPERTS, tm), lambda i: (0, i))),
        out_shape=(jax.ShapeDtypeStruct((T, D_MODEL), F32), _tile_rows_shape(T),
                   jax.ShapeDtypeStruct((N_EXPERTS, T), F32)),
        compiler_params=_cparams(("parallel",)),
        name="merge",
    )(x2d, hf, hb, gr, ga, gb, yb, w_out, g_ffn, wr_hi, wr_lo, b_router)


def _route_kernel(lt_ref, tri_ref, idx_ref, gate_ref, rank_ref, cnt_ref, carry_s, *, tn):
    i = pl.program_id(0)

    @pl.when(i == 0)
    def _():
        carry_s[...] = jnp.zeros_like(carry_s)

    work = lt_ref[...]
    eidx = lax.broadcasted_iota(jnp.int32, work.shape, 0)
    vals, idxs = [], []
    for _ in range(TOP_K):
        mx = jnp.max(work, axis=0, keepdims=True)
        ix = jnp.min(jnp.where(work == mx, eidx, N_EXPERTS), axis=0, keepdims=True)
        vals.append(mx)
        idxs.append(ix)
        work = jnp.where(eidx == ix, -jnp.inf, work)
    ex = [jnp.exp(v - vals[0]) for v in vals]
    denom = ex[0] + ex[1] + ex[2] + ex[3]
    sel = jnp.zeros(work.shape, F32)
    for ix in idxs:
        sel = sel + (eidx == ix).astype(F32)
    before = jnp.dot(sel.astype(BF16), tri_ref[...], preferred_element_type=F32) + carry_s[:, 0:1]
    for k in range(TOP_K):
        rank = jnp.sum(jnp.where(eidx == idxs[k], before, 0.0), axis=0, keepdims=True)
        rank_ref[k:k + 1, :] = rank.astype(jnp.int32)
        idx_ref[k:k + 1, :] = idxs[k]
        gate_ref[k:k + 1, :] = ex[k] / denom
    carry_s[...] = carry_s[...] + jnp.sum(sel, axis=1, keepdims=True)
    cnt_ref[...] = carry_s[...].astype(jnp.int32)


def _route(logits_t):
    E, T = logits_t.shape
    tn = _tile(T, 1024)
    tri = (lax.broadcasted_iota(jnp.int32, (tn, tn), 0) < lax.broadcasted_iota(jnp.int32, (tn, tn), 1)).astype(BF16)
    kt = pl.BlockSpec((TOP_K, tn), lambda i: (0, i))
    kern = functools.partial(_route_kernel, tn=tn)
    return pl.pallas_call(
        kern,
        grid=(T // tn,),
        in_specs=[pl.BlockSpec((E, tn), lambda i: (0, i)), pl.BlockSpec((tn, tn), lambda i: (0, 0))],
        out_specs=(kt, kt, kt, pl.BlockSpec((E, LANES), lambda i: (0, 0))),
        out_shape=(jax.ShapeDtypeStruct((TOP_K, T), jnp.int32), jax.ShapeDtypeStruct((TOP_K, T), F32),
                   jax.ShapeDtypeStruct((TOP_K, T), jnp.int32), jax.ShapeDtypeStruct((E, LANES), jnp.int32)),
        scratch_shapes=[pltpu.VMEM((E, LANES), F32)],
        compiler_params=_cparams(("arbitrary",)),
        name="route",
    )(logits_t, tri)


def _row_copy(src, src_row, dst, dst_row, sem):
    tile = lambda r: pl.ds(pl.multiple_of(r * ROW_TILES, ROW_TILES), ROW_TILES)
    return pltpu.make_async_copy(src.at[tile(src_row)], dst.at[tile(dst_row)], sem)


def _wait_rows(src, dst, n_rows, sem):
    n = pl.multiple_of(n_rows * ROW_TILES, ROW_TILES)
    pltpu.make_async_copy(src.at[pl.ds(0, n)], dst.at[pl.ds(0, n)], sem).wait()


def _scatter_kernel(fill_lo_ref, fill_hi_ref, dest_ref, xn_ref, xs_ref, zero_s, sem, zsem, *, tt):
    i = pl.program_id(0)

    @pl.when(i == 0)
    def _():
        zero_s[...] = jnp.zeros_like(zero_s)

        def per_expert(e, n):
            lo, hi = fill_lo_ref[e], fill_hi_ref[e]

            def fill(r, c):
                _row_copy(zero_s, 0, xs_ref, r, zsem).start()
                return c

            lax.fori_loop(lo, hi, fill, 0)
            return n + (hi - lo)

        n_fill = lax.fori_loop(0, N_EXPERTS, per_expert, 0)

        @pl.when(n_fill > 0)
        def _():
            _wait_rows(xs_ref, xs_ref, n_fill, zsem)

    def body(t, c):
        for k in range(TOP_K):
            _row_copy(xn_ref, t, xs_ref, dest_ref[t * TOP_K + k], sem).start(priority=k % 2)
        return c

    lax.fori_loop(0, tt, body, 0, unroll=2)
    _wait_rows(xs_ref, xs_ref, TOP_K * tt, sem)


def _scatter_rows(fill_lo, fill_hi, dest_flat, xn, n_rows):
    T = xn.shape[0] // ROW_TILES
    tt = _tile(T, 512)
    kern = functools.partial(_scatter_kernel, tt=tt)
    return pl.pallas_call(
        kern,
        grid_spec=pltpu.PrefetchScalarGridSpec(
            num_scalar_prefetch=2,
            grid=(T // tt,),
            in_specs=[pl.BlockSpec((TOP_K * tt,), lambda i, lo, hi: (i,), memory_space=pltpu.SMEM),
                      _tile_rows_spec(tt, lambda i, lo, hi: (i, 0))],
            out_specs=pl.BlockSpec(memory_space=pl.ANY),
            scratch_shapes=[pltpu.VMEM((ROW_TILES, LANES), F32), pltpu.SemaphoreType.DMA(()),
                            pltpu.SemaphoreType.DMA(())],
        ),
        out_shape=_tile_rows_shape(n_rows),
        compiler_params=_cparams(("arbitrary",)),
        name="scatter",
    )(fill_lo, fill_hi, dest_flat, xn)


def _split_w1_kernel(w_ref, perm_ref, wg_ref, wl_ref):
    z = jnp.dot(w_ref[0].astype(BF16), perm_ref[...], preferred_element_type=F32)
    wg_ref[0] = z[:, :D_MODEL].astype(BF16)
    wl_ref[0] = z[:, D_MODEL:].astype(BF16)


def _split_w1(w1):
    E, D, D2 = w1.shape
    tr = _tile(D, 512)
    col = lax.broadcasted_iota(jnp.int32, (D2, D2), 1)
    row = lax.broadcasted_iota(jnp.int32, (D2, D2), 0)
    perm = (row == jnp.where(col < D2 // 2, 2 * col, 2 * (col - D2 // 2) + 1)).astype(BF16)
    out = pl.BlockSpec((1, tr, D2 // 2), lambda e, i: (e, i, 0))
    return pl.pallas_call(
        _split_w1_kernel,
        grid=(E, D // tr),
        in_specs=[pl.BlockSpec((1, tr, D2), lambda e, i: (e, i, 0)),
                  pl.BlockSpec((D2, D2), lambda e, i: (0, 0), pipeline_mode=pl.Buffered(1))],
        out_specs=(out, out),
        out_shape=(jax.ShapeDtypeStruct((E, D, D2 // 2), BF16),) * 2,
        compiler_params=_cparams(("parallel", "parallel")),
        name="split_w1",
    )(w1, perm)


def _ffn_kernel(be_ref, nu_ref, xs_ref, w1g_ref, w1l_ref, b1g_ref, b1l_ref, w2_ref, b2_ref, ys_ref):
    i = pl.program_id(0)

    @pl.when(i < nu_ref[0])
    def _():
        x = _load_tile_rows(xs_ref, 0, ROWS_PER_BLOCK).astype(BF16)
        hg = jnp.dot(x, w1g_ref[0], preferred_element_type=F32) + b1g_ref[0]
        hl = jnp.dot(x, w1l_ref[0], preferred_element_type=F32) + b1l_ref[0]
        xg = jnp.minimum(hg, SWIGLU_LIMIT)
        xl = jnp.clip(hl, -SWIGLU_LIMIT, SWIGLU_LIMIT)
        act = xg * _sigmoid(SWIGLU_ALPHA * xg) * (xl + 1.0)
        _store_tile_rows(ys_ref, jnp.dot(act.astype(BF16), w2_ref[0], preferred_element_type=F32) + b2_ref[0])

    @pl.when(i >= nu_ref[0])
    def _():
        ys_ref[...] = jnp.zeros_like(ys_ref)


def _ffn(block_e, n_used, xs, w1g, w1l, b1g, b1l, w2, b2):
    n_rows = xs.shape[0] // ROW_TILES
    n_blocks = n_rows // ROWS_PER_BLOCK
    rows = _tile_rows_spec(ROWS_PER_BLOCK, lambda i, be, nu: (i, 0))
    rows_in = _tile_rows_spec(ROWS_PER_BLOCK, lambda i, be, nu: (jnp.minimum(i, nu[0] - 1), 0))
    wspec = pl.BlockSpec((1, D_MODEL, D_MODEL), lambda i, be, nu: (be[i], 0, 0))
    bspec = pl.BlockSpec((1, 1, D_MODEL), lambda i, be, nu: (be[i], 0, 0))
    return pl.pallas_call(
        _ffn_kernel,
        grid_spec=pltpu.PrefetchScalarGridSpec(
            num_scalar_prefetch=2,
            grid=(n_blocks,),
            in_specs=[rows_in, wspec, wspec, bspec, bspec, wspec, bspec],
            out_specs=rows,
        ),
        out_shape=_tile_rows_shape(n_rows),
        compiler_params=_cparams(("arbitrary",)),
        name="ffn",
    )(block_e, n_used, xs, w1g, w1l, b1g, b1l, w2, b2)


def _final_kernel(dest_ref, h1_ref, ys_ref, gate_ref, p_ref, gpl_ref, wpg_ref, wpe_ref, gfin_ref,
                  out_ref, ybuf, sems, *, tm):
    half = tm // 2

    def issue(h):
        def body(t, c):
            tt = h * half + t
            for k in range(TOP_K):
                _row_copy(ys_ref, dest_ref[tt * TOP_K + k], ybuf, k * tm + tt, sems.at[h]).start(priority=k % 2)
            return c
        lax.fori_loop(0, half, body, 0, unroll=2)

    def finish(h):
        _wait_rows(ys_ref, ybuf, TOP_K * half, sems.at[h])
        rows = pl.ds(h * half, half)
        h2 = h1_ref[rows, :]
        for k in range(TOP_K):
            h2 = h2 + gate_ref[rows, k:k + 1] * _load_tile_rows(ybuf, k * tm + h * half, half)
        hn = _rms(h2, gpl_ref[...]).astype(BF16)
        gate = _sigmoid(jnp.dot(hn, wpg_ref[...], preferred_element_type=F32))
        pe = jnp.dot(p_ref[rows, :].astype(BF16), wpe_ref[...], preferred_element_type=F32)
        out_ref[rows, :] = _rms(h2 + gate * pe, gfin_ref[...])

    issue(0)
    issue(1)
    finish(0)
    finish(1)


def _final(dest_flat, h1, ys, gates_tk, p2d, g_pl, w_pg, w_pe, g_final):
    T = h1.shape[0]
    tm = _tile(T, 256)
    tok = pl.BlockSpec((tm, D_MODEL), lambda i: (i, 0))
    const = lambda shape: pl.BlockSpec(shape, lambda i: (0,) * len(shape))
    kern = functools.partial(_final_kernel, tm=tm)
    return pl.pallas_call(
        kern,
        grid=(T // tm,),
        in_specs=[pl.BlockSpec((TOP_K * tm,), lambda i: (i,), memory_space=pltpu.SMEM),
                  tok, pl.BlockSpec(memory_space=pl.ANY),
                  pl.BlockSpec((tm, TOP_K), lambda i: (i, 0)), pl.BlockSpec((tm, PLE_DIM), lambda i: (i, 0)),
                  const(g_pl.shape), const(w_pg.shape), const(w_pe.shape), const(g_final.shape)],
        out_specs=tok,
        out_shape=jax.ShapeDtypeStruct((T, D_MODEL), F32),
        scratch_shapes=[pltpu.VMEM((TOP_K * tm * ROW_TILES, LANES), F32), pltpu.SemaphoreType.DMA((2,))],
        compiler_params=_cparams(("arbitrary",)),
        name="final",
    )(dest_flat, h1, ys, gates_tk, p2d, g_pl, w_pg, w_pe, g_final)


def _rope_tables(S):
    pos = jnp.arange(S, dtype=F32)
    inv = ROPE_THETA ** (-jnp.arange(0, HEAD_DIM, 2, dtype=F32) / HEAD_DIM)
    ang = pos[:, None] * inv[None, :]
    ang = jnp.concatenate([ang, ang, ang, ang], axis=-1)
    sign = jnp.where((jnp.arange(LANES) % HEAD_DIM) < HEAD_DIM // 2, -1.0, 1.0).astype(F32)
    return jnp.cos(ang), jnp.sin(ang) * sign


def _block_diag_groups(w):
    per = LRU_GROUP // LRU_BLOCK
    g = w.shape[0] // per
    eye = jnp.eye(per, dtype=w.dtype)
    return jnp.einsum('gbij,bc->gbicj', w.reshape(g, per, LRU_BLOCK, LRU_BLOCK), eye).reshape(g, LRU_GROUP, LRU_GROUP)


def _prep_weights(g_mix, w_in, conv_w, conv_b, lru_wa, lru_ba, lru_wx, lru_bx, lru_lambda,
                  lam_q1, lam_k1, lam_q2, lam_k2, subln_g, w_out, g_ffn, w_router, b_router,
                  w1, b1, w2, b2, g_pl, w_pg, w_pe, g_final):
    l = 0
    W = D_MODEL
    w = w_in[l]
    pw = dict(
        g_mix=g_mix[l][None, :],
        w_rest=jnp.concatenate([w[:, :4 * W], w[:, 5 * W:]], axis=1).astype(BF16),
        w_vt=w[:, 4 * W:5 * W].T.astype(BF16),
        conv_w=conv_w[l], conv_b=conv_b[l][None, :],
        lam_vecs=tuple(v[l][None, :] for v in (lam_q1, lam_k1, lam_q2, lam_k2)),
        subln_g=subln_g[l][:, None],
        w_out=w_out[l].astype(BF16), g_ffn=g_ffn[l][None, :],
        b_router=b_router[l][:, None],
        b1g=b1[l][:, None, 0::2], b1l=b1[l][:, None, 1::2],
        w2=w2[l].astype(BF16), b2=b2[l][:, None, :],
        g_pl=g_pl[l][None, :], w_pg=w_pg[l].astype(BF16), w_pe=w_pe[l].astype(BF16),
        g_final=g_final[None, :],
    )
    pw['w1g'], pw['w1l'] = _split_w1(w1[l])
    wr_t = w_router[l].T
    wr_hi = wr_t.astype(BF16)
    pw['wr_hi'] = wr_hi
    pw['wr_lo'] = (wr_t - wr_hi.astype(F32)).astype(BF16)
    for d in range(2):
        wa = _block_diag_groups(lru_wa[l, d])
        wx = _block_diag_groups(lru_wx[l, d])
        pw[f'lru_wg{d}'] = jnp.concatenate([wa, wx], axis=2).astype(BF16)
        ng = D_MODEL // LRU_GROUP
        pw[f'lru_bg{d}'] = jnp.concatenate([lru_ba[l, d].reshape(ng, 1, LRU_GROUP),
                                            lru_bx[l, d].reshape(ng, 1, LRU_GROUP)], axis=2)
        pw[f'lru_lam{d}'] = lru_lambda[l, d][None, :]
    return pw


def _moe_plan(idx, rank, counts, T):
    padded = ((counts + ROWS_PER_BLOCK - 1) // ROWS_PER_BLOCK) * ROWS_PER_BLOCK
    pad_end = jnp.cumsum(padded)
    pad_start = pad_end - padded
    onehot = idx[..., None] == jnp.arange(N_EXPERTS, dtype=idx.dtype)
    dest = jnp.sum(jnp.where(onehot, pad_start.astype(idx.dtype), 0), axis=-1) + rank
    n_blocks = -(-(T * TOP_K) // ROWS_PER_BLOCK) + N_EXPERTS
    block_start = jnp.arange(n_blocks, dtype=pad_end.dtype) * ROWS_PER_BLOCK
    block_e = jnp.minimum(jnp.sum(pad_end[None, :] <= block_start[:, None], axis=1),
                          N_EXPERTS - 1).astype(jnp.int32)
    n_used = (pad_end[-1:] // ROWS_PER_BLOCK).astype(jnp.int32)
    dest_flat = dest.T.reshape(-1)
    n_rows = n_blocks * ROWS_PER_BLOCK
    fill_lo = (pad_start + counts).astype(jnp.int32)
    fill_hi = pad_end.astype(jnp.int32).at[N_EXPERTS - 1].set(n_rows)
    return dest_flat, fill_lo, fill_hi, block_e, n_used, n_rows


def _trunk(x, p, pw):
    B, S, _ = x.shape
    T = B * S
    x2d = x.reshape(T, D_MODEL)
    cos_t, sin_t = _rope_tables(S)
    xr, gr, q, k, vt, ga, gb = _inproj(x2d, pw['g_mix'], cos_t, sin_t, pw['w_rest'], pw['w_vt'], S)
    xr3 = xr.reshape(B, S, D_MODEL)
    hf = _lru(xr3, pw['conv_w'], pw['conv_b'], pw['lru_wg0'], pw['lru_bg0'], pw['lru_lam0'], False)
    hb = _lru(xr3, pw['conv_w'], pw['conv_b'], pw['lru_wg1'], pw['lru_bg1'], pw['lru_lam1'], True)
    yb = _attn(q.reshape(B, S, D_MODEL), k.reshape(B, S, D_MODEL), vt, pw['lam_vecs'], pw['subln_g'])
    h1, xn, logits_t = _merge(x2d, hf.reshape(T, D_MODEL), hb.reshape(T, D_MODEL), gr, ga, gb,
                              yb.reshape(T, D_MODEL), pw['w_out'], pw['g_ffn'],
                              pw['wr_hi'], pw['wr_lo'], pw['b_router'])
    idx, gates, rank, cnt = _route(logits_t)
    dest_flat, fill_lo, fill_hi, block_e, n_used, n_rows = _moe_plan(idx, rank, cnt[:, 0], T)
    xs = _scatter_rows(fill_lo, fill_hi, dest_flat, xn, n_rows)
    ys = _ffn(block_e, n_used, xs, pw['w1g'], pw['w1l'], pw['b1g'], pw['b1l'], pw['w2'], pw['b2'])
    out = _final(dest_flat, h1, ys, gates.T, p[0].reshape(T, PLE_DIM),
                 pw['g_pl'], pw['w_pg'], pw['w_pe'], pw['g_final'])
    return out.reshape(B, S, D_MODEL)


def kernel(x_prompt, x_sample, p_prompt, p_sample, g_mix, w_in, conv_w, conv_b, lru_wa, lru_ba, lru_wx, lru_bx, lru_lambda, lam_q1, lam_k1, lam_q2, lam_k2, subln_g, w_out, g_ffn, w_router, b_router, w1, b1, w2, b2, g_pl, w_pg, w_pe, g_final):
    pw = _prep_weights(g_mix, w_in, conv_w, conv_b, lru_wa, lru_ba, lru_wx, lru_bx, lru_lambda,
                       lam_q1, lam_k1, lam_q2, lam_k2, subln_g, w_out, g_ffn, w_router, b_router,
                       w1, b1, w2, b2, g_pl, w_pg, w_pe, g_final)
    return (_trunk(x_prompt, p_prompt, pw), _trunk(x_sample, p_sample, pw))
```

```python
import functools
import math

import jax
import jax.numpy as jnp
from jax import lax
from jax.experimental import pallas as pl
from jax.experimental.pallas import tpu as pltpu

F32 = jnp.float32
BF16 = jnp.bfloat16

D_MODEL = 1024
LRU_BLOCK = 64
LRU_C = 8.0
CONV_W = 4
CONV_LEFT = 2
N_HEADS = 8
HEAD_DIM = 64
V_DIM = 128
ROPE_THETA = 10000.0
N_EXPERTS = 32
TOP_K = 4
SWIGLU_LIMIT = 7.0
SWIGLU_ALPHA = 1.702
ROWS_PER_BLOCK = 512
PLE_DIM = 256
EPS = 1e-6
LAM_INIT = 0.8 - 0.6 * math.exp(-0.3 * 0)
LOG2E = 1.4426950408889634

LANES = 128
SUBLANES = 8
BF16_ROWS = 16
MXU_N = 256
LRU_GROUP = MXU_N
VMEM_LIMIT = 56 * 1024 * 1024


def _tile(n, pref):
    t = min(n, pref)
    while n % t:
        t //= 2
    return t


def _cparams(sem):
    return pltpu.CompilerParams(dimension_semantics=sem, vmem_limit_bytes=VMEM_LIMIT)


def _rms(x, g):
    return x * lax.rsqrt(jnp.mean(x * x, axis=-1, keepdims=True) + EPS) * g


def _sigmoid(x):
    return 1.0 / (1.0 + jnp.exp(-x))


ROW_TILES = D_MODEL // LANES


def _tile_rows_shape(n):
    return jax.ShapeDtypeStruct((n * ROW_TILES, LANES), F32)


def _tile_rows_spec(n, index_map):
    return pl.BlockSpec((n * ROW_TILES, LANES), index_map)


def _store_tile_rows(ref, x, first_row=0):
    n = x.shape[0]
    for s in range(ROW_TILES):
        ref[pl.ds(first_row * ROW_TILES + s, n, stride=ROW_TILES), :] = x[:, s * LANES:(s + 1) * LANES]


def _load_tile_rows(ref, first_row, n):
    return jnp.concatenate([ref[pl.ds(first_row * ROW_TILES + s, n, stride=ROW_TILES), :]
                            for s in range(ROW_TILES)], axis=1)


def _inproj_kernel(x_ref, g_ref, cos_ref, sin_ref, w_ref, wvt_ref,
                   xr_ref, gr_ref, q_ref, k_ref, vt_ref, ga_ref, gb_ref):
    u = _rms(x_ref[...], g_ref[...]).astype(BF16)

    def proj(seg):
        return jnp.dot(u, w_ref[:, seg * D_MODEL:(seg + 1) * D_MODEL], preferred_element_type=F32)

    reps = D_MODEL // LANES
    cos = jnp.concatenate([cos_ref[...]] * reps, axis=1)
    sin = jnp.concatenate([sin_ref[...]] * reps, axis=1)
    lane = lax.broadcasted_iota(jnp.int32, cos.shape, 1)
    first_half = (lane % HEAD_DIM) < (HEAD_DIM // 2)

    def rope(z):
        swapped = jnp.where(first_half,
                            pltpu.roll(z, D_MODEL - HEAD_DIM // 2, 1),
                            pltpu.roll(z, HEAD_DIM // 2, 1))
        return z * cos + swapped * sin

    xr_ref[...] = proj(0).astype(BF16)
    gr_ref[...] = proj(1).astype(BF16)
    q_ref[...] = (rope(proj(2)) * (HEAD_DIM ** -0.5 * LOG2E)).astype(BF16)
    k_ref[...] = rope(proj(3)).astype(BF16)
    ga_ref[...] = proj(4).astype(BF16)
    gb_ref[...] = proj(5).astype(BF16)
    vt = lax.dot_general(wvt_ref[...], u, (((1,), (1,)), ((), ())), preferred_element_type=F32)
    vt_ref[...] = vt.astype(BF16)


def _inproj(x2d, g_mix, cos_t, sin_t, w_rest, w_vt, seq):
    T = x2d.shape[0]
    tm = _tile(seq, 512)
    n_pos = seq // tm
    tok = pl.BlockSpec((tm, D_MODEL), lambda i: (i, 0))
    pos = pl.BlockSpec((tm, LANES), lambda i: (i % n_pos, 0))
    const = lambda shape: pl.BlockSpec(shape, lambda i: (0,) * len(shape), pipeline_mode=pl.Buffered(1))
    outs = [jax.ShapeDtypeStruct((T, D_MODEL), BF16)] * 4
    out_shape = (outs[0], outs[1], outs[2], outs[3],
                 jax.ShapeDtypeStruct((D_MODEL, T), BF16), outs[0], outs[0])
    return pl.pallas_call(
        _inproj_kernel,
        grid=(T // tm,),
        in_specs=[tok, const((1, D_MODEL)), pos, pos,
                  const(w_rest.shape), const(w_vt.shape)],
        out_specs=(tok, tok, tok, tok, pl.BlockSpec((D_MODEL, tm), lambda i: (0, i)), tok, tok),
        out_shape=out_shape,
        compiler_params=_cparams(("parallel",)),
        name="inproj",
    )(x2d, g_mix, cos_t, sin_t, w_rest, w_vt)


def _lru_kernel(xm_ref, xp_ref, xn_ref, cw_ref, cb_ref, wg_ref, bg_ref, lam_ref,
                h_ref, a_s, b_s, carry_s, *, reverse, tm, n_tiles):
    i = pl.program_id(1)
    ti = (n_tiles - 1 - i) if reverse else i
    C = D_MODEL

    @pl.when(i == 0)
    def _():
        carry_s[...] = jnp.zeros_like(carry_s)

    xm = xm_ref[0].astype(F32)
    xp = xp_ref[0].astype(F32) * (ti > 0).astype(F32)
    xn = xn_ref[0].astype(F32) * (ti < n_tiles - 1).astype(F32)
    big = jnp.concatenate([xp, xm, xn], axis=0)
    xc = jnp.broadcast_to(cb_ref[...], (tm, C))
    n_big = tm + 2 * BF16_ROWS
    for j in range(CONV_W):
        shift = (CONV_LEFT - j) % n_big
        shifted = big if shift == 0 else pltpu.roll(big, shift, 0)
        xc = xc + shifted[BF16_ROWS:BF16_ROWS + tm, :] * cw_ref[j:j + 1, :]

    xcb = xc.astype(BF16)
    r_parts, i_parts = [], []
    for g in range(C // LRU_GROUP):
        z = jnp.dot(xcb[:, g * LRU_GROUP:(g + 1) * LRU_GROUP], wg_ref[g], preferred_element_type=F32)
        z = z + bg_ref[g]
        r_parts.append(z[:, :LRU_GROUP])
        i_parts.append(z[:, LRU_GROUP:])
    r = _sigmoid(jnp.concatenate(r_parts, axis=1))
    ig = _sigmoid(jnp.concatenate(i_parts, axis=1))

    neg_lam = -lam_ref[...]
    softplus = jnp.maximum(neg_lam, 0.0) + jnp.log1p(jnp.exp(-jnp.abs(neg_lam)))
    a = jnp.exp((-LRU_C * softplus) * r)
    y = 1.0 - a * a
    a_s[...] = a
    b_s[...] = (y * lax.rsqrt(jnp.maximum(y, 1e-30))) * (ig * xc)

    n_slab = tm // SUBLANES
    row = lax.broadcasted_iota(jnp.int32, (SUBLANES, C), 0)

    def body(s, carry):
        sl = (n_slab - 1 - s) if reverse else s
        r0 = pl.multiple_of(sl * SUBLANES, SUBLANES)
        a8 = a_s[pl.ds(r0, SUBLANES), :]
        b8 = b_s[pl.ds(r0, SUBLANES), :]
        for d in (1, 2, 4):
            if reverse:
                valid = row < SUBLANES - d
                shift = SUBLANES - d
            else:
                valid = row >= d
                shift = d
            a_sh = jnp.where(valid, pltpu.roll(a8, shift, 0), 1.0)
            b_sh = jnp.where(valid, pltpu.roll(b8, shift, 0), 0.0)
            b8 = a8 * b_sh + b8
            a8 = a8 * a_sh
        h8 = b8 + a8 * carry
        b_s[pl.ds(r0, SUBLANES), :] = h8
        edge = h8[0:1, :] if reverse else h8[SUBLANES - 1:SUBLANES, :]
        return jnp.broadcast_to(edge, (SUBLANES, C))

    carry_s[...] = lax.fori_loop(0, n_slab, body, carry_s[...], unroll=4)
    h_ref[0] = b_s[...].astype(BF16)


def _lru(xr3, conv_w, conv_b, wg, bg, lam, reverse):
    B, S, C = xr3.shape
    tm = _tile(S, 512)
    n_tiles = S // tm
    hb = tm // BF16_ROWS
    n_halo = S // BF16_ROWS
    tidx = (lambda i: n_tiles - 1 - i) if reverse else (lambda i: i)
    const = lambda shape: pl.BlockSpec(shape, lambda b, i: (0,) * len(shape))
    kern = functools.partial(_lru_kernel, reverse=reverse, tm=tm, n_tiles=n_tiles)
    return pl.pallas_call(
        kern,
        grid=(B, n_tiles),
        in_specs=[
            pl.BlockSpec((1, tm, C), lambda b, i: (b, tidx(i), 0)),
            pl.BlockSpec((1, BF16_ROWS, C), lambda b, i: (b, jnp.maximum(tidx(i) * hb - 1, 0), 0)),
            pl.BlockSpec((1, BF16_ROWS, C), lambda b, i: (b, jnp.minimum((tidx(i) + 1) * hb, n_halo - 1), 0)),
            const(conv_w.shape), const(conv_b.shape), const(wg.shape), const(bg.shape), const(lam.shape),
        ],
        out_specs=pl.BlockSpec((1, tm, C), lambda b, i: (b, tidx(i), 0)),
        out_shape=jax.ShapeDtypeStruct((B, S, C), BF16),
        scratch_shapes=[pltpu.VMEM((tm, C), F32), pltpu.VMEM((tm, C), F32), pltpu.VMEM((SUBLANES, C), F32)],
        compiler_params=_cparams(("parallel", "arbitrary")),
        name="lru_bwd" if reverse else "lru_fwd",
    )(xr3, xr3, xr3, conv_w, conv_b, wg, bg, lam)


def _attn_kernel(lq1_ref, lk1_ref, lq2_ref, lk2_ref, q_ref, k_ref, vt_ref, g_ref,
                 o_ref, qz_s, s_s, m_s, l_s, acc_s, *, tq, tk, seq, unroll):
    qt = q_ref[0].astype(F32).T
    chan = lax.broadcasted_iota(jnp.int32, qt.shape, 0)
    qz_s[:, 0:tq] = jnp.where(chan < HEAD_DIM, qt, 0.0).astype(BF16)
    qz_s[:, tq:2 * tq] = jnp.where(chan >= HEAD_DIM, qt, 0.0).astype(BF16)
    m_s[...] = jnp.full_like(m_s, -1e30)
    l_s[...] = jnp.zeros_like(l_s)
    acc_s[...] = jnp.zeros_like(acc_s)
    n_chunks = seq // tk

    n_ct = 2 * tq // MXU_N

    def scores(kc, slot, ct):
        cols = slice(ct * MXU_N, (ct + 1) * MXU_N)
        s_s[slot, ct] = jnp.dot(kc, qz_s[:, cols], preferred_element_type=F32)

    def accumulate(vc, slot, ct):
        cols = slice(ct * MXU_N, (ct + 1) * MXU_N)
        s = s_s[slot, ct]
        m_old = m_s[:, cols]
        m_new = jnp.maximum(m_old, jnp.max(s, axis=0, keepdims=True))
        alpha = jnp.exp2(m_old - m_new)
        p = jnp.exp2(s - m_new)
        pv = jnp.dot(vc, p.astype(BF16), preferred_element_type=F32)
        l_s[:, cols] = alpha * l_s[:, cols] + jnp.sum(p, axis=0, keepdims=True)
        acc_s[:, cols] = alpha * acc_s[:, cols] + pv
        m_s[:, cols] = m_new

    def keys(j):
        return k_ref[0, pl.ds(pl.multiple_of(j * tk, tk), tk), :]

    def values(j):
        return vt_ref[:, pl.ds(pl.multiple_of(j * tk, tk), tk)]

    kc = keys(0)
    for ct in range(n_ct):
        scores(kc, 0, ct)

    def body(jj, c):
        for u in range(unroll):
            j = unroll * jj + u
            kc, va = keys(jnp.minimum(j + 1, n_chunks - 1)), values(j)
            for ct in range(n_ct):
                scores(kc, (u + 1) % 2, ct)
                accumulate(va, u % 2, ct)
        return c

    lax.fori_loop(0, n_chunks // unroll, body, 0)

    lam = (jnp.exp(jnp.sum(lq1_ref[...] * lk1_ref[...], axis=1, keepdims=True))
           - jnp.exp(jnp.sum(lq2_ref[...] * lk2_ref[...], axis=1, keepdims=True)) + LAM_INIT)
    o_all = acc_s[...] / l_s[...]
    o = o_all[:, :tq] - lam * o_all[:, tq:]
    y = o * lax.rsqrt(jnp.mean(o * o, axis=0, keepdims=True) + EPS) * g_ref[...] * (1.0 - LAM_INIT)
    o_ref[0] = y.T.astype(BF16)


def _attn(q3, k3, vt, lam_vecs, subln_g):
    B, S, _ = q3.shape
    tq = _tile(S, 1024)
    tk = _tile(S, 512)
    unroll = max(u for u in (2, 4, 8) if (S // tk) % u == 0)
    vec = pl.BlockSpec((1, HEAD_DIM), lambda b, h, i: (0, 0))
    kern = functools.partial(_attn_kernel, tq=tq, tk=tk, seq=S, unroll=unroll)
    return pl.pallas_call(
        kern,
        grid=(B, N_HEADS, S // tq),
        in_specs=[vec, vec, vec, vec,
                  pl.BlockSpec((1, tq, V_DIM), lambda b, h, i: (b, i, h)),
                  pl.BlockSpec((1, S, V_DIM), lambda b, h, i: (b, 0, h)),
                  pl.BlockSpec((V_DIM, S), lambda b, h, i: (h, b)),
                  pl.BlockSpec((V_DIM, 1), lambda b, h, i: (0, 0))],
        out_specs=pl.BlockSpec((1, tq, V_DIM), lambda b, h, i: (b, i, h)),
        out_shape=jax.ShapeDtypeStruct((B, S, N_HEADS * V_DIM), BF16),
        scratch_shapes=[pltpu.VMEM((V_DIM, 2 * tq), BF16), pltpu.VMEM((2, 2 * tq // MXU_N, tk, MXU_N), F32),
                        pltpu.VMEM((1, 2 * tq), F32), pltpu.VMEM((1, 2 * tq), F32),
                        pltpu.VMEM((V_DIM, 2 * tq), F32)],
        compiler_params=_cparams(("parallel", "parallel", "arbitrary")),
        name="attn",
    )(*lam_vecs, q3, k3, vt, subln_g)


def _gelu_tanh(x):
    return 0.5 * x * (1.0 + jnp.tanh(math.sqrt(2.0 / math.pi) * (x + 0.044715 * (x * x * x))))


def _merge_kernel(x_ref, hf_ref, hb_ref, gr_ref, ga_ref, gb_ref, yb_ref, wout_ref, g_ref,
                  wrh_ref, wrl_ref, br_ref, h1_ref, xn_ref, lt_ref):
    f = lambda ref: ref[...].astype(F32)
    ya = _gelu_tanh(f(gr_ref)) * (f(hf_ref) + f(hb_ref))
    mixed = _sigmoid(f(ga_ref)) * ya + _sigmoid(f(gb_ref)) * f(yb_ref)
    h1 = x_ref[...] + jnp.dot(mixed.astype(BF16), wout_ref[...], preferred_element_type=F32)
    h1_ref[...] = h1
    xn = _rms(h1, g_ref[...])
    _store_tile_rows(xn_ref, xn)
    xn_hi = xn.astype(BF16)
    xn_lo = (xn - xn_hi.astype(F32)).astype(BF16)
    nt = lambda w, v: lax.dot_general(w, v, (((1,), (1,)), ((), ())), preferred_element_type=F32)
    lt_ref[...] = nt(wrh_ref[...], xn_hi) + nt(wrh_ref[...], xn_lo) + nt(wrl_ref[...], xn_hi) + br_ref[...]


def _merge(x2d, hf, hb, gr, ga, gb, yb, w_out, g_ffn, wr_hi, wr_lo, b_router):
    T = x2d.shape[0]
    tm = _tile(T, 512)
    tok = pl.BlockSpec((tm, D_MODEL), lambda i: (i, 0))
    const = lambda shape: pl.BlockSpec(shape, lambda i: (0,) * len(shape))
    return pl.pallas_call(
        _merge_kernel,
        grid=(T // tm,),
        in_specs=[tok] * 7 + [const(w_out.shape), const(g_ffn.shape), const(wr_hi.shape),
                              const(wr_lo.shape), const(b_router.shape)],
        out_specs=(tok, _tile_rows_spec(tm, lambda i: (i, 0)), pl.BlockSpec((N_EXPERTS, tm), lambda i: (0, i))),
        out_shape=(jax.ShapeDtypeStruct((T, D_MODEL), F32), _tile_rows_shape(T),
                   jax.ShapeDtypeStruct((N_EXPERTS, T), F32)),
        compiler_params=_cparams(("parallel",)),
        name="merge",
    )(x2d, hf, hb, gr, ga, gb, yb, w_out, g_ffn, wr_hi, wr_lo, b_router)


def _route_kernel(lt_ref, tri_ref, idx_ref, gate_ref, rank_ref, cnt_ref, carry_s, *, tn):
    i = pl.program_id(0)

    @pl.when(i == 0)
    def _():
        carry_s[...] = jnp.zeros_like(carry_s)

    work = lt_ref[...]
    eidx = lax.broadcasted_iota(jnp.int32, work.shape, 0)
    vals, idxs = [], []
    for _ in range(TOP_K):
        mx = jnp.max(work, axis=0, keepdims=True)
        ix = jnp.min(jnp.where(work == mx, eidx, N_EXPERTS), axis=0, keepdims=True)
        vals.append(mx)
        idxs.append(ix)
        work = jnp.where(eidx == ix, -jnp.inf, work)
    ex = [jnp.exp(v - vals[0]) for v in vals]
    denom = ex[0] + ex[1] + ex[2] + ex[3]
    sel = jnp.zeros(work.shape, F32)
    for ix in idxs:
        sel = sel + (eidx == ix).astype(F32)
    before = jnp.dot(sel.astype(BF16), tri_ref[...], preferred_element_type=F32) + carry_s[:, 0:1]
    for k in range(TOP_K):
        rank = jnp.sum(jnp.where(eidx == idxs[k], before, 0.0), axis=0, keepdims=True)
        rank_ref[k:k + 1, :] = rank.astype(jnp.int32)
        idx_ref[k:k + 1, :] = idxs[k]
        gate_ref[k:k + 1, :] = ex[k] / denom
    carry_s[...] = carry_s[...] + jnp.sum(sel, axis=1, keepdims=True)
    cnt_ref[...] = carry_s[...].astype(jnp.int32)


def _route(logits_t):
    E, T = logits_t.shape
    tn = _tile(T, 1024)
    tri = (lax.broadcasted_iota(jnp.int32, (tn, tn), 0) < lax.broadcasted_iota(jnp.int32, (tn, tn), 1)).astype(BF16)
    kt = pl.BlockSpec((TOP_K, tn), lambda i: (0, i))
    kern = functools.partial(_route_kernel, tn=tn)
    return pl.pallas_call(
        kern,
        grid=(T // tn,),
        in_specs=[pl.BlockSpec((E, tn), lambda i: (0, i)), pl.BlockSpec((tn, tn), lambda i: (0, 0))],
        out_specs=(kt, kt, kt, pl.BlockSpec((E, LANES), lambda i: (0, 0))),
        out_shape=(jax.ShapeDtypeStruct((TOP_K, T), jnp.int32), jax.ShapeDtypeStruct((TOP_K, T), F32),
                   jax.ShapeDtypeStruct((TOP_K, T), jnp.int32), jax.ShapeDtypeStruct((E, LANES), jnp.int32)),
        scratch_shapes=[pltpu.VMEM((E, LANES), F32)],
        compiler_params=_cparams(("arbitrary",)),
        name="route",
    )(logits_t, tri)


def _row_copy(src, src_row, dst, dst_row, sem):
    tile = lambda r: pl.ds(pl.multiple_of(r * ROW_TILES, ROW_TILES), ROW_TILES)
    return pltpu.make_async_copy(src.at[tile(src_row)], dst.at[tile(dst_row)], sem)


def _wait_rows(src, dst, n_rows, sem):
    n = pl.multiple_of(n_rows * ROW_TILES, ROW_TILES)
    pltpu.make_async_copy(src.at[pl.ds(0, n)], dst.at[pl.ds(0, n)], sem).wait()


def _scatter_kernel(fill_lo_ref, fill_hi_ref, dest_ref, xn_ref, xs_ref, zero_s, sem, zsem, *, tt):
    i = pl.program_id(0)

    @pl.when(i == 0)
    def _():
        zero_s[...] = jnp.zeros_like(zero_s)

        def per_expert(e, n):
            lo, hi = fill_lo_ref[e], fill_hi_ref[e]

            def fill(r, c):
                _row_copy(zero_s, 0, xs_ref, r, zsem).start()
                return c

            lax.fori_loop(lo, hi, fill, 0)
            return n + (hi - lo)

        n_fill = lax.fori_loop(0, N_EXPERTS, per_expert, 0)

        @pl.when(n_fill > 0)
        def _():
            _wait_rows(xs_ref, xs_ref, n_fill, zsem)

    def body(t, c):
        for k in range(TOP_K):
            _row_copy(xn_ref, t, xs_ref, dest_ref[t * TOP_K + k], sem).start(priority=k % 2)
        return c

    lax.fori_loop(0, tt, body, 0, unroll=2)
    _wait_rows(xs_ref, xs_ref, TOP_K * tt, sem)


def _scatter_rows(fill_lo, fill_hi, dest_flat, xn, n_rows):
    T = xn.shape[0] // ROW_TILES
    tt = _tile(T, 2048)
    kern = functools.partial(_scatter_kernel, tt=tt)
    return pl.pallas_call(
        kern,
        grid_spec=pltpu.PrefetchScalarGridSpec(
            num_scalar_prefetch=2,
            grid=(T // tt,),
            in_specs=[pl.BlockSpec((TOP_K * tt,), lambda i, lo, hi: (i,), memory_space=pltpu.SMEM),
                      _tile_rows_spec(tt, lambda i, lo, hi: (i, 0))],
            out_specs=pl.BlockSpec(memory_space=pl.ANY),
            scratch_shapes=[pltpu.VMEM((ROW_TILES, LANES), F32), pltpu.SemaphoreType.DMA(()),
                            pltpu.SemaphoreType.DMA(())],
        ),
        out_shape=_tile_rows_shape(n_rows),
        compiler_params=_cparams(("arbitrary",)),
        name="scatter",
    )(fill_lo, fill_hi, dest_flat, xn)


def _split_w1_kernel(w_ref, perm_ref, wg_ref, wl_ref):
    half = MXU_N // 2
    for g in range(w_ref.shape[2] // MXU_N):
        z = jnp.dot(w_ref[0, :, g * MXU_N:(g + 1) * MXU_N].astype(BF16), perm_ref[...],
                    preferred_element_type=F32)
        wg_ref[0, :, g * half:(g + 1) * half] = z[:, :half].astype(BF16)
        wl_ref[0, :, g * half:(g + 1) * half] = z[:, half:].astype(BF16)


def _split_w1(w1):
    E, D, D2 = w1.shape
    tr = _tile(D, 512)
    col = lax.broadcasted_iota(jnp.int32, (MXU_N, MXU_N), 1)
    row = lax.broadcasted_iota(jnp.int32, (MXU_N, MXU_N), 0)
    perm = (row == jnp.where(col < MXU_N // 2, 2 * col, 2 * (col - MXU_N // 2) + 1)).astype(BF16)
    out = pl.BlockSpec((1, tr, D2 // 2), lambda e, i: (e, i, 0))
    return pl.pallas_call(
        _split_w1_kernel,
        grid=(E, D // tr),
        in_specs=[pl.BlockSpec((1, tr, D2), lambda e, i: (e, i, 0)),
                  pl.BlockSpec((MXU_N, MXU_N), lambda e, i: (0, 0))],
        out_specs=(out, out),
        out_shape=(jax.ShapeDtypeStruct((E, D, D2 // 2), BF16),) * 2,
        compiler_params=_cparams(("parallel", "parallel")),
        name="split_w1",
    )(w1, perm)


def _ffn_kernel(be_ref, nu_ref, xs_ref, w1g_ref, w1l_ref, b1g_ref, b1l_ref, w2_ref, b2_ref, ys_ref):
    i = pl.program_id(0)

    @pl.when(i < nu_ref[0])
    def _():
        x = _load_tile_rows(xs_ref, 0, ROWS_PER_BLOCK).astype(BF16)
        hg = jnp.dot(x, w1g_ref[0], preferred_element_type=F32) + b1g_ref[0]
        hl = jnp.dot(x, w1l_ref[0], preferred_element_type=F32) + b1l_ref[0]
        xg = jnp.minimum(hg, SWIGLU_LIMIT)
        xl = jnp.clip(hl, -SWIGLU_LIMIT, SWIGLU_LIMIT)
        act = xg * _sigmoid(SWIGLU_ALPHA * xg) * (xl + 1.0)
        _store_tile_rows(ys_ref, jnp.dot(act.astype(BF16), w2_ref[0], preferred_element_type=F32) + b2_ref[0])

    @pl.when(i >= nu_ref[0])
    def _():
        ys_ref[...] = jnp.zeros_like(ys_ref)


def _ffn(block_e, n_used, xs, w1g, w1l, b1g, b1l, w2, b2):
    n_rows = xs.shape[0] // ROW_TILES
    n_blocks = n_rows // ROWS_PER_BLOCK
    rows = _tile_rows_spec(ROWS_PER_BLOCK, lambda i, be, nu: (i, 0))
    rows_in = _tile_rows_spec(ROWS_PER_BLOCK, lambda i, be, nu: (jnp.minimum(i, nu[0] - 1), 0))
    wspec = pl.BlockSpec((1, D_MODEL, D_MODEL), lambda i, be, nu: (be[i], 0, 0))
    bspec = pl.BlockSpec((1, 1, D_MODEL), lambda i, be, nu: (be[i], 0, 0))
    return pl.pallas_call(
        _ffn_kernel,
        grid_spec=pltpu.PrefetchScalarGridSpec(
            num_scalar_prefetch=2,
            grid=(n_blocks,),
            in_specs=[rows_in, wspec, wspec, bspec, bspec, wspec, bspec],
            out_specs=rows,
        ),
        out_shape=_tile_rows_shape(n_rows),
        compiler_params=_cparams(("arbitrary",)),
        name="ffn",
    )(block_e, n_used, xs, w1g, w1l, b1g, b1l, w2, b2)


def _final_kernel(dest_ref, h1_ref, ys_ref, gate_ref, p_ref, gpl_ref, wpg_ref, wpe_ref, gfin_ref,
                  out_ref, ybuf, sems, *, tm):
    half = tm // 2

    def issue(h):
        def body(t, c):
            tt = h * half + t
            for k in range(TOP_K):
                _row_copy(ys_ref, dest_ref[tt * TOP_K + k], ybuf, k * tm + tt, sems.at[h]).start(priority=k % 2)
            return c
        lax.fori_loop(0, half, body, 0, unroll=2)

    def finish(h):
        _wait_rows(ys_ref, ybuf, TOP_K * half, sems.at[h])
        rows = pl.ds(h * half, half)
        h2 = h1_ref[rows, :]
        for k in range(TOP_K):
            h2 = h2 + gate_ref[rows, k:k + 1] * _load_tile_rows(ybuf, k * tm + h * half, half)
        hn = _rms(h2, gpl_ref[...]).astype(BF16)
        gate = _sigmoid(jnp.dot(hn, wpg_ref[...], preferred_element_type=F32))
        pe = jnp.dot(p_ref[rows, :].astype(BF16), wpe_ref[...], preferred_element_type=F32)
        out_ref[rows, :] = _rms(h2 + gate * pe, gfin_ref[...])

    issue(0)
    issue(1)
    finish(0)
    finish(1)


def _final(dest_flat, h1, ys, gates_tk, p2d, g_pl, w_pg, w_pe, g_final):
    T = h1.shape[0]
    tm = _tile(T, 512)
    tok = pl.BlockSpec((tm, D_MODEL), lambda i: (i, 0))
    const = lambda shape: pl.BlockSpec(shape, lambda i: (0,) * len(shape))
    kern = functools.partial(_final_kernel, tm=tm)
    return pl.pallas_call(
        kern,
        grid=(T // tm,),
        in_specs=[pl.BlockSpec((TOP_K * tm,), lambda i: (i,), memory_space=pltpu.SMEM),
                  tok, pl.BlockSpec(memory_space=pl.ANY),
                  pl.BlockSpec((tm, TOP_K), lambda i: (i, 0)), pl.BlockSpec((tm, PLE_DIM), lambda i: (i, 0)),
                  const(g_pl.shape), const(w_pg.shape), const(w_pe.shape), const(g_final.shape)],
        out_specs=tok,
        out_shape=jax.ShapeDtypeStruct((T, D_MODEL), F32),
        scratch_shapes=[pltpu.VMEM((TOP_K * tm * ROW_TILES, LANES), F32), pltpu.SemaphoreType.DMA((2,))],
        compiler_params=_cparams(("arbitrary",)),
        name="final",
    )(dest_flat, h1, ys, gates_tk, p2d, g_pl, w_pg, w_pe, g_final)


def _rope_tables(S):
    pos = jnp.arange(S, dtype=F32)
    inv = ROPE_THETA ** (-jnp.arange(0, HEAD_DIM, 2, dtype=F32) / HEAD_DIM)
    ang = pos[:, None] * inv[None, :]
    ang = jnp.concatenate([ang, ang, ang, ang], axis=-1)
    sign = jnp.where((jnp.arange(LANES) % HEAD_DIM) < HEAD_DIM // 2, -1.0, 1.0).astype(F32)
    return jnp.cos(ang), jnp.sin(ang) * sign


def _block_diag_groups(w):
    per = LRU_GROUP // LRU_BLOCK
    g = w.shape[0] // per
    eye = jnp.eye(per, dtype=w.dtype)
    return jnp.einsum('gbij,bc->gbicj', w.reshape(g, per, LRU_BLOCK, LRU_BLOCK), eye).reshape(g, LRU_GROUP, LRU_GROUP)


def _prep_weights(g_mix, w_in, conv_w, conv_b, lru_wa, lru_ba, lru_wx, lru_bx, lru_lambda,
                  lam_q1, lam_k1, lam_q2, lam_k2, subln_g, w_out, g_ffn, w_router, b_router,
                  w1, b1, w2, b2, g_pl, w_pg, w_pe, g_final):
    l = 0
    W = D_MODEL
    w = w_in[l]
    pw = dict(
        g_mix=g_mix[l][None, :],
        w_rest=jnp.concatenate([w[:, :4 * W], w[:, 5 * W:]], axis=1).astype(BF16),
        w_vt=w[:, 4 * W:5 * W].T.astype(BF16),
        conv_w=conv_w[l], conv_b=conv_b[l][None, :],
        lam_vecs=tuple(v[l][None, :] for v in (lam_q1, lam_k1, lam_q2, lam_k2)),
        subln_g=subln_g[l][:, None],
        w_out=w_out[l].astype(BF16), g_ffn=g_ffn[l][None, :],
        b_router=b_router[l][:, None],
        b1g=b1[l][:, None, 0::2], b1l=b1[l][:, None, 1::2],
        w2=w2[l].astype(BF16), b2=b2[l][:, None, :],
        g_pl=g_pl[l][None, :], w_pg=w_pg[l].astype(BF16), w_pe=w_pe[l].astype(BF16),
        g_final=g_final[None, :],
    )
    pw['w1g'], pw['w1l'] = _split_w1(w1[l])
    wr_t = w_router[l].T
    wr_hi = wr_t.astype(BF16)
    pw['wr_hi'] = wr_hi
    pw['wr_lo'] = (wr_t - wr_hi.astype(F32)).astype(BF16)
    for d in range(2):
        wa = _block_diag_groups(lru_wa[l, d])
        wx = _block_diag_groups(lru_wx[l, d])
        pw[f'lru_wg{d}'] = jnp.concatenate([wa, wx], axis=2).astype(BF16)
        ng = D_MODEL // LRU_GROUP
        pw[f'lru_bg{d}'] = jnp.concatenate([lru_ba[l, d].reshape(ng, 1, LRU_GROUP),
                                            lru_bx[l, d].reshape(ng, 1, LRU_GROUP)], axis=2)
        pw[f'lru_lam{d}'] = lru_lambda[l, d][None, :]
    return pw


def _moe_plan(idx, rank, counts, T):
    padded = ((counts + ROWS_PER_BLOCK - 1) // ROWS_PER_BLOCK) * ROWS_PER_BLOCK
    pad_end = jnp.cumsum(padded)
    pad_start = pad_end - padded
    onehot = idx[..., None] == jnp.arange(N_EXPERTS, dtype=idx.dtype)
    dest = jnp.sum(jnp.where(onehot, pad_start.astype(idx.dtype), 0), axis=-1) + rank
    n_blocks = -(-(T * TOP_K) // ROWS_PER_BLOCK) + N_EXPERTS
    block_start = jnp.arange(n_blocks, dtype=pad_end.dtype) * ROWS_PER_BLOCK
    block_e = jnp.minimum(jnp.sum(pad_end[None, :] <= block_start[:, None], axis=1),
                          N_EXPERTS - 1).astype(jnp.int32)
    n_used = (pad_end[-1:] // ROWS_PER_BLOCK).astype(jnp.int32)
    dest_flat = dest.T.reshape(-1)
    n_rows = n_blocks * ROWS_PER_BLOCK
    fill_lo = (pad_start + counts).astype(jnp.int32)
    fill_hi = pad_end.astype(jnp.int32).at[N_EXPERTS - 1].set(n_rows)
    return dest_flat, fill_lo, fill_hi, block_e, n_used, n_rows


def _trunk(x, p, pw, cos_t, sin_t):
    B, S, _ = x.shape
    T = B * S
    x2d = x.reshape(T, D_MODEL)
    xr, gr, q, k, vt, ga, gb = _inproj(x2d, pw['g_mix'], cos_t, sin_t, pw['w_rest'], pw['w_vt'], S)
    xr3 = xr.reshape(B, S, D_MODEL)
    hf = _lru(xr3, pw['conv_w'], pw['conv_b'], pw['lru_wg0'], pw['lru_bg0'], pw['lru_lam0'], False)
    hb = _lru(xr3, pw['conv_w'], pw['conv_b'], pw['lru_wg1'], pw['lru_bg1'], pw['lru_lam1'], True)
    yb = _attn(q.reshape(B, S, D_MODEL), k.reshape(B, S, D_MODEL), vt, pw['lam_vecs'], pw['subln_g'])
    h1, xn, logits_t = _merge(x2d, hf.reshape(T, D_MODEL), hb.reshape(T, D_MODEL), gr, ga, gb,
                              yb.reshape(T, D_MODEL), pw['w_out'], pw['g_ffn'],
                              pw['wr_hi'], pw['wr_lo'], pw['b_router'])
    idx, gates, rank, cnt = _route(logits_t)
    dest_flat, fill_lo, fill_hi, block_e, n_used, n_rows = _moe_plan(idx, rank, cnt[:, 0], T)
    xs = _scatter_rows(fill_lo, fill_hi, dest_flat, xn, n_rows)
    ys = _ffn(block_e, n_used, xs, pw['w1g'], pw['w1l'], pw['b1g'], pw['b1l'], pw['w2'], pw['b2'])
    out = _final(dest_flat, h1, ys, gates.T, p[0].reshape(T, PLE_DIM),
                 pw['g_pl'], pw['w_pg'], pw['w_pe'], pw['g_final'])
    return out.reshape(B, S, D_MODEL)


def kernel(x_prompt, x_sample, p_prompt, p_sample, g_mix, w_in, conv_w, conv_b, lru_wa, lru_ba, lru_wx, lru_bx, lru_lambda, lam_q1, lam_k1, lam_q2, lam_k2, subln_g, w_out, g_ffn, w_router, b_router, w1, b1, w2, b2, g_pl, w_pg, w_pe, g_final):
    pw = _prep_weights(g_mix, w_in, conv_w, conv_b, lru_wa, lru_ba, lru_wx, lru_bx, lru_lambda,
                       lam_q1, lam_k1, lam_q2, lam_k2, subln_g, w_out, g_ffn, w_router, b_router,
                       w1, b1, w2, b2, g_pl, w_pg, w_pe, g_final)
    cos_t, sin_t = _rope_tables(max(x_prompt.shape[1], x_sample.shape[1]))
    return (_trunk(x_prompt, p_prompt, pw, cos_t, sin_t), _trunk(x_sample, p_sample, pw, cos_t, sin_t))
```

```python
import functools
import math

import jax
import jax.numpy as jnp
from jax import lax
from jax.experimental import pallas as pl
from jax.experimental.pallas import tpu as pltpu

F32 = jnp.float32
BF16 = jnp.bfloat16

D_MODEL = 1024
LRU_BLOCK = 64
LRU_C = 8.0
CONV_W = 4
CONV_LEFT = 2
N_HEADS = 8
HEAD_DIM = 64
V_DIM = 128
ROPE_THETA = 10000.0
N_EXPERTS = 32
TOP_K = 4
SWIGLU_LIMIT = 7.0
SWIGLU_ALPHA = 1.702
ROWS_PER_BLOCK = 512
PLE_DIM = 256
EPS = 1e-6
LAM_INIT = 0.8 - 0.6 * math.exp(-0.3 * 0)
LOG2E = 1.4426950408889634
SCORE_FLOOR = -1e30
F32_TINY = 1e-30

LANES = 128
SUBLANES = 8
BF16_ROWS = 16
MXU_N = 256
LRU_GROUP = MXU_N
VMEM_LIMIT = 56 * 1024 * 1024


def _tile(n, pref):
    t = min(n, pref)
    while n % t:
        t //= 2
    return t


def _cparams(sem):
    return pltpu.CompilerParams(dimension_semantics=sem, vmem_limit_bytes=VMEM_LIMIT)


def _rms(x, g):
    return x * lax.rsqrt(jnp.mean(x * x, axis=-1, keepdims=True) + EPS) * g


def _sigmoid(x):
    return 1.0 / (1.0 + jnp.exp(-x))


ROW_TILES = D_MODEL // LANES


def _tile_rows_shape(n):
    return jax.ShapeDtypeStruct((n * ROW_TILES, LANES), F32)


def _tile_rows_spec(n, index_map):
    return pl.BlockSpec((n * ROW_TILES, LANES), index_map)


def _store_tile_rows(ref, x, first_row=0):
    n = x.shape[0]
    for s in range(ROW_TILES):
        ref[pl.ds(first_row * ROW_TILES + s, n, stride=ROW_TILES), :] = x[:, s * LANES:(s + 1) * LANES]


def _load_tile_rows(ref, first_row, n):
    return jnp.concatenate([ref[pl.ds(first_row * ROW_TILES + s, n, stride=ROW_TILES), :]
                            for s in range(ROW_TILES)], axis=1)


def _inproj_kernel(x_ref, g_ref, cos_ref, sin_ref, w_ref, wvt_ref,
                   xr_ref, gr_ref, q_ref, k_ref, vt_ref, ga_ref, gb_ref):
    u = _rms(x_ref[...], g_ref[...]).astype(BF16)

    def proj(seg):
        return jnp.dot(u, w_ref[:, seg * D_MODEL:(seg + 1) * D_MODEL], preferred_element_type=F32)

    reps = D_MODEL // LANES
    cos = jnp.concatenate([cos_ref[...]] * reps, axis=1)
    sin = jnp.concatenate([sin_ref[...]] * reps, axis=1)
    lane = lax.broadcasted_iota(jnp.int32, cos.shape, 1)
    first_half = (lane % HEAD_DIM) < (HEAD_DIM // 2)

    def rope(z):
        swapped = jnp.where(first_half,
                            pltpu.roll(z, D_MODEL - HEAD_DIM // 2, 1),
                            pltpu.roll(z, HEAD_DIM // 2, 1))
        return z * cos + swapped * sin

    xr_ref[...] = proj(0).astype(BF16)
    gr_ref[...] = proj(1).astype(BF16)
    q_ref[...] = (rope(proj(2)) * (HEAD_DIM ** -0.5 * LOG2E)).astype(BF16)
    k_ref[...] = rope(proj(3)).astype(BF16)
    ga_ref[...] = proj(4).astype(BF16)
    gb_ref[...] = proj(5).astype(BF16)
    vt = lax.dot_general(wvt_ref[...], u, (((1,), (1,)), ((), ())), preferred_element_type=F32)
    vt_ref[...] = vt.astype(BF16)


def _inproj(x2d, g_mix, cos_t, sin_t, w_rest, w_vt, seq):
    T = x2d.shape[0]
    tm = _tile(seq, 512)
    n_pos = seq // tm
    tok = pl.BlockSpec((tm, D_MODEL), lambda i: (i, 0))
    pos = pl.BlockSpec((tm, LANES), lambda i: (i % n_pos, 0))
    const = lambda shape: pl.BlockSpec(shape, lambda i: (0,) * len(shape), pipeline_mode=pl.Buffered(1))
    outs = [jax.ShapeDtypeStruct((T, D_MODEL), BF16)] * 4
    out_shape = (outs[0], outs[1], outs[2], outs[3],
                 jax.ShapeDtypeStruct((D_MODEL, T), BF16), outs[0], outs[0])
    return pl.pallas_call(
        _inproj_kernel,
        grid=(T // tm,),
        in_specs=[tok, const((1, D_MODEL)), pos, pos,
                  const(w_rest.shape), const(w_vt.shape)],
        out_specs=(tok, tok, tok, tok, pl.BlockSpec((D_MODEL, tm), lambda i: (0, i)), tok, tok),
        out_shape=out_shape,
        compiler_params=_cparams(("parallel",)),
        name="inproj",
    )(x2d, g_mix, cos_t, sin_t, w_rest, w_vt)


def _lru_kernel(xm_ref, xp_ref, xn_ref, cw_ref, cb_ref, wg_ref, bg_ref, lam_ref,
                h_ref, a_s, b_s, carry_s, *, reverse, tm, n_tiles):
    i = pl.program_id(1)
    ti = (n_tiles - 1 - i) if reverse else i
    C = D_MODEL

    @pl.when(i == 0)
    def _():
        carry_s[...] = jnp.zeros_like(carry_s)

    xm = xm_ref[0].astype(F32)
    xp = xp_ref[0].astype(F32) * (ti > 0).astype(F32)
    xn = xn_ref[0].astype(F32) * (ti < n_tiles - 1).astype(F32)
    big = jnp.concatenate([xp, xm, xn], axis=0)
    xc = jnp.broadcast_to(cb_ref[...], (tm, C))
    n_big = tm + 2 * BF16_ROWS
    for j in range(CONV_W):
        shift = (CONV_LEFT - j) % n_big
        shifted = big if shift == 0 else pltpu.roll(big, shift, 0)
        xc = xc + shifted[BF16_ROWS:BF16_ROWS + tm, :] * cw_ref[j:j + 1, :]

    xcb = xc.astype(BF16)
    r_parts, i_parts = [], []
    for g in range(C // LRU_GROUP):
        z = jnp.dot(xcb[:, g * LRU_GROUP:(g + 1) * LRU_GROUP], wg_ref[g], preferred_element_type=F32)
        z = z + bg_ref[g]
        r_parts.append(z[:, :LRU_GROUP])
        i_parts.append(z[:, LRU_GROUP:])
    r = _sigmoid(jnp.concatenate(r_parts, axis=1))
    ig = _sigmoid(jnp.concatenate(i_parts, axis=1))

    neg_lam = -lam_ref[...]
    softplus = jnp.maximum(neg_lam, 0.0) + jnp.log1p(jnp.exp(-jnp.abs(neg_lam)))
    a = jnp.exp((-LRU_C * softplus) * r)
    y = 1.0 - a * a
    a_s[...] = a
    b_s[...] = (y * lax.rsqrt(jnp.maximum(y, F32_TINY))) * (ig * xc)

    n_slab = tm // SUBLANES
    row = lax.broadcasted_iota(jnp.int32, (SUBLANES, C), 0)

    def body(s, carry):
        sl = (n_slab - 1 - s) if reverse else s
        r0 = pl.multiple_of(sl * SUBLANES, SUBLANES)
        a8 = a_s[pl.ds(r0, SUBLANES), :]
        b8 = b_s[pl.ds(r0, SUBLANES), :]
        for d in (1, 2, 4):
            if reverse:
                valid = row < SUBLANES - d
                shift = SUBLANES - d
            else:
                valid = row >= d
                shift = d
            a_sh = jnp.where(valid, pltpu.roll(a8, shift, 0), 1.0)
            b_sh = jnp.where(valid, pltpu.roll(b8, shift, 0), 0.0)
            b8 = a8 * b_sh + b8
            a8 = a8 * a_sh
        h8 = b8 + a8 * carry
        b_s[pl.ds(r0, SUBLANES), :] = h8
        edge = h8[0:1, :] if reverse else h8[SUBLANES - 1:SUBLANES, :]
        return jnp.broadcast_to(edge, (SUBLANES, C))

    carry_s[...] = lax.fori_loop(0, n_slab, body, carry_s[...], unroll=4)
    h_ref[0] = b_s[...].astype(BF16)


def _lru(xr3, conv_w, conv_b, wg, bg, lam, reverse):
    B, S, C = xr3.shape
    tm = _tile(S, 512)
    n_tiles = S // tm
    hb = tm // BF16_ROWS
    n_halo = S // BF16_ROWS
    tidx = (lambda i: n_tiles - 1 - i) if reverse else (lambda i: i)
    const = lambda shape: pl.BlockSpec(shape, lambda b, i: (0,) * len(shape))
    kern = functools.partial(_lru_kernel, reverse=reverse, tm=tm, n_tiles=n_tiles)
    return pl.pallas_call(
        kern,
        grid=(B, n_tiles),
        in_specs=[
            pl.BlockSpec((1, tm, C), lambda b, i: (b, tidx(i), 0)),
            pl.BlockSpec((1, BF16_ROWS, C), lambda b, i: (b, jnp.maximum(tidx(i) * hb - 1, 0), 0)),
            pl.BlockSpec((1, BF16_ROWS, C), lambda b, i: (b, jnp.minimum((tidx(i) + 1) * hb, n_halo - 1), 0)),
            const(conv_w.shape), const(conv_b.shape), const(wg.shape), const(bg.shape), const(lam.shape),
        ],
        out_specs=pl.BlockSpec((1, tm, C), lambda b, i: (b, tidx(i), 0)),
        out_shape=jax.ShapeDtypeStruct((B, S, C), BF16),
        scratch_shapes=[pltpu.VMEM((tm, C), F32), pltpu.VMEM((tm, C), F32), pltpu.VMEM((SUBLANES, C), F32)],
        compiler_params=_cparams(("parallel", "arbitrary")),
        name="lru_bwd" if reverse else "lru_fwd",
    )(xr3, xr3, xr3, conv_w, conv_b, wg, bg, lam)


def _attn_kernel(lq1_ref, lk1_ref, lq2_ref, lk2_ref, q_ref, k_ref, vt_ref, g_ref,
                 o_ref, qz_s, s_s, m_s, l_s, acc_s, *, tq, tk, seq, unroll):
    qt = q_ref[0].astype(F32).T
    chan = lax.broadcasted_iota(jnp.int32, qt.shape, 0)
    qz_s[:, 0:tq] = jnp.where(chan < HEAD_DIM, qt, 0.0).astype(BF16)
    qz_s[:, tq:2 * tq] = jnp.where(chan >= HEAD_DIM, qt, 0.0).astype(BF16)
    m_s[...] = jnp.full_like(m_s, SCORE_FLOOR)
    l_s[...] = jnp.zeros_like(l_s)
    acc_s[...] = jnp.zeros_like(acc_s)
    n_chunks = seq // tk

    n_ct = 2 * tq // MXU_N

    def scores(kc, slot, ct):
        cols = slice(ct * MXU_N, (ct + 1) * MXU_N)
        s_s[slot, ct] = jnp.dot(kc, qz_s[:, cols], preferred_element_type=F32)

    def accumulate(vc, slot, ct):
        cols = slice(ct * MXU_N, (ct + 1) * MXU_N)
        s = s_s[slot, ct]
        m_old = m_s[:, cols]
        m_new = jnp.maximum(m_old, jnp.max(s, axis=0, keepdims=True))
        alpha = jnp.exp2(m_old - m_new)
        p = jnp.exp2(s - m_new)
        pv = jnp.dot(vc, p.astype(BF16), preferred_element_type=F32)
        l_s[:, cols] = alpha * l_s[:, cols] + jnp.sum(p, axis=0, keepdims=True)
        acc_s[:, cols] = alpha * acc_s[:, cols] + pv
        m_s[:, cols] = m_new

    def keys(j):
        return k_ref[0, pl.ds(pl.multiple_of(j * tk, tk), tk), :]

    def values(j):
        return vt_ref[:, pl.ds(pl.multiple_of(j * tk, tk), tk)]

    kc = keys(0)
    for ct in range(n_ct):
        scores(kc, 0, ct)

    def body(jj, c):
        for u in range(unroll):
            j = unroll * jj + u
            kc, va = keys(jnp.minimum(j + 1, n_chunks - 1)), values(j)
            for ct in range(n_ct):
                scores(kc, (u + 1) % 2, ct)
                accumulate(va, u % 2, ct)
        return c

    lax.fori_loop(0, n_chunks // unroll, body, 0)

    lam = (jnp.exp(jnp.sum(lq1_ref[...] * lk1_ref[...], axis=1, keepdims=True))
           - jnp.exp(jnp.sum(lq2_ref[...] * lk2_ref[...], axis=1, keepdims=True)) + LAM_INIT)
    o_all = acc_s[...] / l_s[...]
    o = o_all[:, :tq] - lam * o_all[:, tq:]
    y = o * lax.rsqrt(jnp.mean(o * o, axis=0, keepdims=True) + EPS) * g_ref[...] * (1.0 - LAM_INIT)
    o_ref[0] = y.T.astype(BF16)


def _attn(q3, k3, vt, lam_vecs, subln_g):
    B, S, _ = q3.shape
    tq = _tile(S, 1024)
    tk = _tile(S, 512)
    unroll = max(u for u in (2, 4, 8) if (S // tk) % u == 0)
    vec = pl.BlockSpec((1, HEAD_DIM), lambda b, h, i: (0, 0))
    kern = functools.partial(_attn_kernel, tq=tq, tk=tk, seq=S, unroll=unroll)
    return pl.pallas_call(
        kern,
        grid=(B, N_HEADS, S // tq),
        in_specs=[vec, vec, vec, vec,
                  pl.BlockSpec((1, tq, V_DIM), lambda b, h, i: (b, i, h)),
                  pl.BlockSpec((1, S, V_DIM), lambda b, h, i: (b, 0, h)),
                  pl.BlockSpec((V_DIM, S), lambda b, h, i: (h, b)),
                  pl.BlockSpec((V_DIM, 1), lambda b, h, i: (0, 0))],
        out_specs=pl.BlockSpec((1, tq, V_DIM), lambda b, h, i: (b, i, h)),
        out_shape=jax.ShapeDtypeStruct((B, S, N_HEADS * V_DIM), BF16),
        scratch_shapes=[pltpu.VMEM((V_DIM, 2 * tq), BF16), pltpu.VMEM((2, 2 * tq // MXU_N, tk, MXU_N), F32),
                        pltpu.VMEM((1, 2 * tq), F32), pltpu.VMEM((1, 2 * tq), F32),
                        pltpu.VMEM((V_DIM, 2 * tq), F32)],
        compiler_params=_cparams(("parallel", "parallel", "arbitrary")),
        name="attn",
    )(*lam_vecs, q3, k3, vt, subln_g)


def _gelu_tanh(x):
    return 0.5 * x * (1.0 + jnp.tanh(math.sqrt(2.0 / math.pi) * (x + 0.044715 * (x * x * x))))


def _merge_kernel(x_ref, hf_ref, hb_ref, gr_ref, ga_ref, gb_ref, yb_ref, wout_ref, g_ref,
                  wrh_ref, wrl_ref, br_ref, h1_ref, xn_ref, lt_ref):
    f = lambda ref: ref[...].astype(F32)
    ya = _gelu_tanh(f(gr_ref)) * (f(hf_ref) + f(hb_ref))
    mixed = _sigmoid(f(ga_ref)) * ya + _sigmoid(f(gb_ref)) * f(yb_ref)
    h1 = x_ref[...] + jnp.dot(mixed.astype(BF16), wout_ref[...], preferred_element_type=F32)
    h1_ref[...] = h1
    xn = _rms(h1, g_ref[...])
    _store_tile_rows(xn_ref, xn)
    xn_hi = xn.astype(BF16)
    xn_lo = (xn - xn_hi.astype(F32)).astype(BF16)
    nt = lambda w, v: lax.dot_general(w, v, (((1,), (1,)), ((), ())), preferred_element_type=F32)
    lt_ref[...] = nt(wrh_ref[...], xn_hi) + nt(wrh_ref[...], xn_lo) + nt(wrl_ref[...], xn_hi) + br_ref[...]


def _merge(x2d, hf, hb, gr, ga, gb, yb, w_out, g_ffn, wr_hi, wr_lo, b_router):
    T = x2d.shape[0]
    tm = _tile(T, 512)
    tok = pl.BlockSpec((tm, D_MODEL), lambda i: (i, 0))
    const = lambda shape: pl.BlockSpec(shape, lambda i: (0,) * len(shape))
    return pl.pallas_call(
        _merge_kernel,
        grid=(T // tm,),
        in_specs=[tok] * 7 + [const(w_out.shape), const(g_ffn.shape), const(wr_hi.shape),
                              const(wr_lo.shape), const(b_router.shape)],
        out_specs=(tok, _tile_rows_spec(tm, lambda i: (i, 0)), pl.BlockSpec((N_EXPERTS, tm), lambda i: (0, i))),
        out_shape=(jax.ShapeDtypeStruct((T, D_MODEL), F32), _tile_rows_shape(T),
                   jax.ShapeDtypeStruct((N_EXPERTS, T), F32)),
        compiler_params=_cparams(("parallel",)),
        name="merge",
    )(x2d, hf, hb, gr, ga, gb, yb, w_out, g_ffn, wr_hi, wr_lo, b_router)


def _route_kernel(lt_ref, tri_ref, idx_ref, gate_ref, rank_ref, cnt_ref, carry_s, *, tn):
    i = pl.program_id(0)

    @pl.when(i == 0)
    def _():
        carry_s[...] = jnp.zeros_like(carry_s)

    work = lt_ref[...]
    eidx = lax.broadcasted_iota(jnp.int32, work.shape, 0)
    vals, idxs = [], []
    for _ in range(TOP_K):
        mx = jnp.max(work, axis=0, keepdims=True)
        ix = jnp.min(jnp.where(work == mx, eidx, N_EXPERTS), axis=0, keepdims=True)
        vals.append(mx)
        idxs.append(ix)
        work = jnp.where(eidx == ix, -jnp.inf, work)
    ex = [jnp.exp(v - vals[0]) for v in vals]
    denom = ex[0] + ex[1] + ex[2] + ex[3]
    sel = jnp.zeros(work.shape, F32)
    for ix in idxs:
        sel = sel + (eidx == ix).astype(F32)
    before = jnp.dot(sel.astype(BF16), tri_ref[...], preferred_element_type=F32) + carry_s[:, 0:1]
    for k in range(TOP_K):
        rank = jnp.sum(jnp.where(eidx == idxs[k], before, 0.0), axis=0, keepdims=True)
        rank_ref[k:k + 1, :] = rank.astype(jnp.int32)
        idx_ref[k:k + 1, :] = idxs[k]
        gate_ref[k:k + 1, :] = ex[k] / denom
    carry_s[...] = carry_s[...] + jnp.sum(sel, axis=1, keepdims=True)
    cnt_ref[...] = carry_s[...].astype(jnp.int32)


def _route(logits_t):
    E, T = logits_t.shape
    tn = _tile(T, 1024)
    tri = (lax.broadcasted_iota(jnp.int32, (tn, tn), 0) < lax.broadcasted_iota(jnp.int32, (tn, tn), 1)).astype(BF16)
    kt = pl.BlockSpec((TOP_K, tn), lambda i: (0, i))
    kern = functools.partial(_route_kernel, tn=tn)
    return pl.pallas_call(
        kern,
        grid=(T // tn,),
        in_specs=[pl.BlockSpec((E, tn), lambda i: (0, i)), pl.BlockSpec((tn, tn), lambda i: (0, 0))],
        out_specs=(kt, kt, kt, pl.BlockSpec((E, LANES), lambda i: (0, 0))),
        out_shape=(jax.ShapeDtypeStruct((TOP_K, T), jnp.int32), jax.ShapeDtypeStruct((TOP_K, T), F32),
                   jax.ShapeDtypeStruct((TOP_K, T), jnp.int32), jax.ShapeDtypeStruct((E, LANES), jnp.int32)),
        scratch_shapes=[pltpu.VMEM((E, LANES), F32)],
        compiler_params=_cparams(("arbitrary",)),
        name="route",
    )(logits_t, tri)


def _row_copy(src, src_row, dst, dst_row, sem):
    tile = lambda r: pl.ds(pl.multiple_of(r * ROW_TILES, ROW_TILES), ROW_TILES)
    return pltpu.make_async_copy(src.at[tile(src_row)], dst.at[tile(dst_row)], sem)


def _wait_rows(src, dst, n_rows, sem):
    n = pl.multiple_of(n_rows * ROW_TILES, ROW_TILES)
    pltpu.make_async_copy(src.at[pl.ds(0, n)], dst.at[pl.ds(0, n)], sem).wait()


def _scatter_kernel(fill_lo_ref, fill_hi_ref, dest_ref, xn_ref, xs_ref, zero_s, sem, zsem, *, tt):
    i = pl.program_id(0)

    @pl.when(i == 0)
    def _():
        zero_s[...] = jnp.zeros_like(zero_s)

        def per_expert(e, n):
            lo, hi = fill_lo_ref[e], fill_hi_ref[e]

            def fill(r, c):
                _row_copy(zero_s, 0, xs_ref, r, zsem).start()
                return c

            lax.fori_loop(lo, hi, fill, 0)
            return n + (hi - lo)

        n_fill = lax.fori_loop(0, N_EXPERTS, per_expert, 0)

        @pl.when(n_fill > 0)
        def _():
            _wait_rows(xs_ref, xs_ref, n_fill, zsem)

    def body(t, c):
        for k in range(TOP_K):
            _row_copy(xn_ref, t, xs_ref, dest_ref[t * TOP_K + k], sem).start(priority=k % 2)
        return c

    lax.fori_loop(0, tt, body, 0, unroll=2)
    _wait_rows(xs_ref, xs_ref, TOP_K * tt, sem)


def _scatter_rows(fill_lo, fill_hi, dest_flat, xn, n_rows):
    T = xn.shape[0] // ROW_TILES
    tt = _tile(T, 2048)
    kern = functools.partial(_scatter_kernel, tt=tt)
    return pl.pallas_call(
        kern,
        grid_spec=pltpu.PrefetchScalarGridSpec(
            num_scalar_prefetch=2,
            grid=(T // tt,),
            in_specs=[pl.BlockSpec((TOP_K * tt,), lambda i, lo, hi: (i,), memory_space=pltpu.SMEM),
                      _tile_rows_spec(tt, lambda i, lo, hi: (i, 0))],
            out_specs=pl.BlockSpec(memory_space=pl.ANY),
            scratch_shapes=[pltpu.VMEM((ROW_TILES, LANES), F32), pltpu.SemaphoreType.DMA(()),
                            pltpu.SemaphoreType.DMA(())],
        ),
        out_shape=_tile_rows_shape(n_rows),
        compiler_params=_cparams(("arbitrary",)),
        name="scatter",
    )(fill_lo, fill_hi, dest_flat, xn)


def _split_w1_kernel(w_ref, perm_ref, wg_ref, wl_ref):
    half = MXU_N // 2
    for g in range(w_ref.shape[2] // MXU_N):
        z = jnp.dot(w_ref[0, :, g * MXU_N:(g + 1) * MXU_N].astype(BF16), perm_ref[...],
                    preferred_element_type=F32)
        wg_ref[0, :, g * half:(g + 1) * half] = z[:, :half].astype(BF16)
        wl_ref[0, :, g * half:(g + 1) * half] = z[:, half:].astype(BF16)


def _split_w1(w1):
    E, D, D2 = w1.shape
    tr = _tile(D, 512)
    col = lax.broadcasted_iota(jnp.int32, (MXU_N, MXU_N), 1)
    row = lax.broadcasted_iota(jnp.int32, (MXU_N, MXU_N), 0)
    perm = (row == jnp.where(col < MXU_N // 2, 2 * col, 2 * (col - MXU_N // 2) + 1)).astype(BF16)
    out = pl.BlockSpec((1, tr, D2 // 2), lambda e, i: (e, i, 0))
    return pl.pallas_call(
        _split_w1_kernel,
        grid=(E, D // tr),
        in_specs=[pl.BlockSpec((1, tr, D2), lambda e, i: (e, i, 0)),
                  pl.BlockSpec((MXU_N, MXU_N), lambda e, i: (0, 0))],
        out_specs=(out, out),
        out_shape=(jax.ShapeDtypeStruct((E, D, D2 // 2), BF16),) * 2,
        compiler_params=_cparams(("parallel", "parallel")),
        name="split_w1",
    )(w1, perm)


def _ffn_kernel(be_ref, nu_ref, xs_ref, w1g_ref, w1l_ref, b1g_ref, b1l_ref, w2_ref, b2_ref, ys_ref):
    i = pl.program_id(0)

    @pl.when(i < nu_ref[0])
    def _():
        x = _load_tile_rows(xs_ref, 0, ROWS_PER_BLOCK).astype(BF16)
        hg = jnp.dot(x, w1g_ref[0], preferred_element_type=F32) + b1g_ref[0]
        hl = jnp.dot(x, w1l_ref[0], preferred_element_type=F32) + b1l_ref[0]
        xg = jnp.minimum(hg, SWIGLU_LIMIT)
        xl = jnp.clip(hl, -SWIGLU_LIMIT, SWIGLU_LIMIT)
        act = xg * _sigmoid(SWIGLU_ALPHA * xg) * (xl + 1.0)
        _store_tile_rows(ys_ref, jnp.dot(act.astype(BF16), w2_ref[0], preferred_element_type=F32) + b2_ref[0])

    @pl.when(i >= nu_ref[0])
    def _():
        ys_ref[...] = jnp.zeros_like(ys_ref)


def _ffn(block_e, n_used, xs, w1g, w1l, b1g, b1l, w2, b2):
    n_rows = xs.shape[0] // ROW_TILES
    n_blocks = n_rows // ROWS_PER_BLOCK
    rows = _tile_rows_spec(ROWS_PER_BLOCK, lambda i, be, nu: (i, 0))
    rows_in = _tile_rows_spec(ROWS_PER_BLOCK, lambda i, be, nu: (jnp.minimum(i, nu[0] - 1), 0))
    wspec = pl.BlockSpec((1, D_MODEL, D_MODEL), lambda i, be, nu: (be[i], 0, 0))
    bspec = pl.BlockSpec((1, 1, D_MODEL), lambda i, be, nu: (be[i], 0, 0))
    return pl.pallas_call(
        _ffn_kernel,
        grid_spec=pltpu.PrefetchScalarGridSpec(
            num_scalar_prefetch=2,
            grid=(n_blocks,),
            in_specs=[rows_in, wspec, wspec, bspec, bspec, wspec, bspec],
            out_specs=rows,
        ),
        out_shape=_tile_rows_shape(n_rows),
        compiler_params=_cparams(("arbitrary",)),
        name="ffn",
    )(block_e, n_used, xs, w1g, w1l, b1g, b1l, w2, b2)


def _final_kernel(dest_ref, next_ref, h1_ref, ys_ref, gate_ref, p_ref, gpl_ref, wpg_ref, wpe_ref, gfin_ref,
                  out_ref, ybuf_a, ybuf_b, sems, *, tm, n_tiles):
    i = pl.program_id(0)
    half = tm // 2

    def issue(idx_ref, h, buf):
        for t in range(half):
            for k in range(TOP_K):
                _row_copy(ys_ref, idx_ref[(h * half + t) * TOP_K + k], buf, k * half + t,
                          sems.at[h]).start(priority=k % 2)

    def combine(h, buf):
        rows = pl.ds(h * half, half)
        h2 = h1_ref[rows, :]
        for k in range(TOP_K):
            h2 = h2 + gate_ref[rows, k:k + 1] * _load_tile_rows(buf, k * half, half)
        hn = _rms(h2, gpl_ref[...]).astype(BF16)
        gate = _sigmoid(jnp.dot(hn, wpg_ref[...], preferred_element_type=F32))
        pe = jnp.dot(p_ref[rows, :].astype(BF16), wpe_ref[...], preferred_element_type=F32)
        out_ref[rows, :] = _rms(h2 + gate * pe, gfin_ref[...])

    @pl.when(i == 0)
    def _():
        issue(dest_ref, 0, ybuf_a)

    _wait_rows(ys_ref, ybuf_a, TOP_K * half, sems.at[0])
    issue(dest_ref, 1, ybuf_b)
    combine(0, ybuf_a)
    _wait_rows(ys_ref, ybuf_b, TOP_K * half, sems.at[1])

    @pl.when(i < n_tiles - 1)
    def _():
        issue(next_ref, 0, ybuf_a)
        combine(1, ybuf_b)

    @pl.when(i == n_tiles - 1)
    def _():
        combine(1, ybuf_b)


def _final(dest_flat, h1, ys, gates_tk, p2d, g_pl, w_pg, w_pe, g_final):
    T = h1.shape[0]
    tm = _tile(T, 256)
    n_tiles = T // tm
    tok = pl.BlockSpec((tm, D_MODEL), lambda i: (i, 0))
    const = lambda shape: pl.BlockSpec(shape, lambda i: (0,) * len(shape))
    kern = functools.partial(_final_kernel, tm=tm, n_tiles=n_tiles)
    half_buf = pltpu.VMEM((TOP_K * (tm // 2) * ROW_TILES, LANES), F32)
    return pl.pallas_call(
        kern,
        grid=(n_tiles,),
        in_specs=[pl.BlockSpec((TOP_K * tm,), lambda i: (i,), memory_space=pltpu.SMEM),
                  pl.BlockSpec((TOP_K * tm,), lambda i: (jnp.minimum(i + 1, n_tiles - 1),),
                               memory_space=pltpu.SMEM),
                  tok, pl.BlockSpec(memory_space=pl.ANY),
                  pl.BlockSpec((tm, TOP_K), lambda i: (i, 0)), pl.BlockSpec((tm, PLE_DIM), lambda i: (i, 0)),
                  const(g_pl.shape), const(w_pg.shape), const(w_pe.shape), const(g_final.shape)],
        out_specs=tok,
        out_shape=jax.ShapeDtypeStruct((T, D_MODEL), F32),
        scratch_shapes=[half_buf, half_buf, pltpu.SemaphoreType.DMA((2,))],
        compiler_params=_cparams(("arbitrary",)),
        name="final",
    )(dest_flat, dest_flat, h1, ys, gates_tk, p2d, g_pl, w_pg, w_pe, g_final)


def _rope_tables(S):
    pos = jnp.arange(S, dtype=F32)
    inv = ROPE_THETA ** (-jnp.arange(0, HEAD_DIM, 2, dtype=F32) / HEAD_DIM)
    ang = pos[:, None] * inv[None, :]
    ang = jnp.concatenate([ang, ang, ang, ang], axis=-1)
    sign = jnp.where((jnp.arange(LANES) % HEAD_DIM) < HEAD_DIM // 2, -1.0, 1.0).astype(F32)
    return jnp.cos(ang), jnp.sin(ang) * sign


def _block_diag_groups(w):
    per = LRU_GROUP // LRU_BLOCK
    g = w.shape[0] // per
    eye = jnp.eye(per, dtype=w.dtype)
    return jnp.einsum('gbij,bc->gbicj', w.reshape(g, per, LRU_BLOCK, LRU_BLOCK), eye).reshape(g, LRU_GROUP, LRU_GROUP)


def _prep_weights(g_mix, w_in, conv_w, conv_b, lru_wa, lru_ba, lru_wx, lru_bx, lru_lambda,
                  lam_q1, lam_k1, lam_q2, lam_k2, subln_g, w_out, g_ffn, w_router, b_router,
                  w1, b1, w2, b2, g_pl, w_pg, w_pe, g_final):
    l = 0
    W = D_MODEL
    w = w_in[l]
    pw = dict(
        g_mix=g_mix[l][None, :],
        w_rest=jnp.concatenate([w[:, :4 * W], w[:, 5 * W:]], axis=1).astype(BF16),
        w_vt=w[:, 4 * W:5 * W].T.astype(BF16),
        conv_w=conv_w[l], conv_b=conv_b[l][None, :],
        lam_vecs=tuple(v[l][None, :] for v in (lam_q1, lam_k1, lam_q2, lam_k2)),
        subln_g=subln_g[l][:, None],
        w_out=w_out[l].astype(BF16), g_ffn=g_ffn[l][None, :],
        b_router=b_router[l][:, None],
        b1g=b1[l][:, None, 0::2], b1l=b1[l][:, None, 1::2],
        w2=w2[l].astype(BF16), b2=b2[l][:, None, :],
        g_pl=g_pl[l][None, :], w_pg=w_pg[l].astype(BF16), w_pe=w_pe[l].astype(BF16),
        g_final=g_final[None, :],
    )
    pw['w1g'], pw['w1l'] = _split_w1(w1[l])
    wr_t = w_router[l].T
    wr_hi = wr_t.astype(BF16)
    pw['wr_hi'] = wr_hi
    pw['wr_lo'] = (wr_t - wr_hi.astype(F32)).astype(BF16)
    for d in range(2):
        wa = _block_diag_groups(lru_wa[l, d])
        wx = _block_diag_groups(lru_wx[l, d])
        pw[f'lru_wg{d}'] = jnp.concatenate([wa, wx], axis=2).astype(BF16)
        ng = D_MODEL // LRU_GROUP
        pw[f'lru_bg{d}'] = jnp.concatenate([lru_ba[l, d].reshape(ng, 1, LRU_GROUP),
                                            lru_bx[l, d].reshape(ng, 1, LRU_GROUP)], axis=2)
        pw[f'lru_lam{d}'] = lru_lambda[l, d][None, :]
    return pw


def _moe_plan(idx, rank, counts, T):
    padded = ((counts + ROWS_PER_BLOCK - 1) // ROWS_PER_BLOCK) * ROWS_PER_BLOCK
    pad_end = jnp.cumsum(padded)
    pad_start = pad_end - padded
    onehot = idx[..., None] == jnp.arange(N_EXPERTS, dtype=idx.dtype)
    dest = jnp.sum(jnp.where(onehot, pad_start.astype(idx.dtype), 0), axis=-1) + rank
    n_blocks = -(-(T * TOP_K) // ROWS_PER_BLOCK) + N_EXPERTS
    block_start = jnp.arange(n_blocks, dtype=pad_end.dtype) * ROWS_PER_BLOCK
    block_e = jnp.minimum(jnp.sum(pad_end[None, :] <= block_start[:, None], axis=1),
                          N_EXPERTS - 1).astype(jnp.int32)
    n_used = (pad_end[-1:] // ROWS_PER_BLOCK).astype(jnp.int32)
    dest_flat = dest.T.reshape(-1)
    n_rows = n_blocks * ROWS_PER_BLOCK
    fill_lo = (pad_start + counts).astype(jnp.int32)
    fill_hi = pad_end.astype(jnp.int32).at[N_EXPERTS - 1].set(n_rows)
    return dest_flat, fill_lo, fill_hi, block_e, n_used, n_rows


def _trunk(x, p, pw, cos_t, sin_t):
    B, S, _ = x.shape
    T = B * S
    x2d = x.reshape(T, D_MODEL)
    xr, gr, q, k, vt, ga, gb = _inproj(x2d, pw['g_mix'], cos_t, sin_t, pw['w_rest'], pw['w_vt'], S)
    xr3 = xr.reshape(B, S, D_MODEL)
    hf = _lru(xr3, pw['conv_w'], pw['conv_b'], pw['lru_wg0'], pw['lru_bg0'], pw['lru_lam0'], False)
    hb = _lru(xr3, pw['conv_w'], pw['conv_b'], pw['lru_wg1'], pw['lru_bg1'], pw['lru_lam1'], True)
    yb = _attn(q.reshape(B, S, D_MODEL), k.reshape(B, S, D_MODEL), vt, pw['lam_vecs'], pw['subln_g'])
    h1, xn, logits_t = _merge(x2d, hf.reshape(T, D_MODEL), hb.reshape(T, D_MODEL), gr, ga, gb,
                              yb.reshape(T, D_MODEL), pw['w_out'], pw['g_ffn'],
                              pw['wr_hi'], pw['wr_lo'], pw['b_router'])
    idx, gates, rank, cnt = _route(logits_t)
    dest_flat, fill_lo, fill_hi, block_e, n_used, n_rows = _moe_plan(idx, rank, cnt[:, 0], T)
    xs = _scatter_rows(fill_lo, fill_hi, dest_flat, xn, n_rows)
    ys = _ffn(block_e, n_used, xs, pw['w1g'], pw['w1l'], pw['b1g'], pw['b1l'], pw['w2'], pw['b2'])
    out = _final(dest_flat, h1, ys, gates.T, p[0].reshape(T, PLE_DIM),
                 pw['g_pl'], pw['w_pg'], pw['w_pe'], pw['g_final'])
    return out.reshape(B, S, D_MODEL)


def kernel(x_prompt, x_sample, p_prompt, p_sample, g_mix, w_in, conv_w, conv_b, lru_wa, lru_ba, lru_wx, lru_bx, lru_lambda, lam_q1, lam_k1, lam_q2, lam_k2, subln_g, w_out, g_ffn, w_router, b_router, w1, b1, w2, b2, g_pl, w_pg, w_pe, g_final):
    pw = _prep_weights(g_mix, w_in, conv_w, conv_b, lru_wa, lru_ba, lru_wx, lru_bx, lru_lambda,
                       lam_q1, lam_k1, lam_q2, lam_k2, subln_g, w_out, g_ffn, w_router, b_router,
                       w1, b1, w2, b2, g_pl, w_pg, w_pe, g_final)
    cos_t, sin_t = _rope_tables(max(x_prompt.shape[1], x_sample.shape[1]))
    return (_trunk(x_prompt, p_prompt, pw, cos_t, sin_t), _trunk(x_sample, p_sample, pw, cos_t, sin_t))
```

```python
import functools
import math

import jax
import jax.numpy as jnp
from jax import lax
from jax.experimental import pallas as pl
from jax.experimental.pallas import tpu as pltpu

F32 = jnp.float32
BF16 = jnp.bfloat16

D_MODEL = 1024
LRU_BLOCK = 64
LRU_C = 8.0
CONV_W = 4
CONV_LEFT = 2
N_HEADS = 8
HEAD_DIM = 64
V_DIM = 128
ROPE_THETA = 10000.0
N_EXPERTS = 32
TOP_K = 4
SWIGLU_LIMIT = 7.0
SWIGLU_ALPHA = 1.702
ROWS_PER_BLOCK = 512
PLE_DIM = 256
EPS = 1e-6
LAM_INIT = 0.8 - 0.6 * math.exp(-0.3 * 0)
LOG2E = 1.4426950408889634
SCORE_FLOOR = -1e30
F32_TINY = 1e-30

LANES = 128
SUBLANES = 8
BF16_ROWS = 16
MXU_N = 256
LRU_GROUP = MXU_N
VMEM_LIMIT = 56 * 1024 * 1024


def _tile(n, pref):
    t = min(n, pref)
    while n % t:
        t //= 2
    return t


def _cparams(sem):
    return pltpu.CompilerParams(dimension_semantics=sem, vmem_limit_bytes=VMEM_LIMIT)


def _rms(x, g):
    return x * lax.rsqrt(jnp.mean(x * x, axis=-1, keepdims=True) + EPS) * g


def _sigmoid(x):
    return 1.0 / (1.0 + jnp.exp(-x))


ROW_TILES = D_MODEL // LANES


def _tile_rows_shape(n):
    return jax.ShapeDtypeStruct((n * ROW_TILES, LANES), F32)


def _tile_rows_spec(n, index_map):
    return pl.BlockSpec((n * ROW_TILES, LANES), index_map)


def _store_tile_rows(ref, x, first_row=0):
    n = x.shape[0]
    for s in range(ROW_TILES):
        ref[pl.ds(first_row * ROW_TILES + s, n, stride=ROW_TILES), :] = x[:, s * LANES:(s + 1) * LANES]


def _load_tile_rows(ref, first_row, n):
    return jnp.concatenate([ref[pl.ds(first_row * ROW_TILES + s, n, stride=ROW_TILES), :]
                            for s in range(ROW_TILES)], axis=1)


def _inproj_kernel(x_ref, g_ref, cos_ref, sin_ref, w_ref, wvt_ref,
                   xr_ref, gr_ref, q_ref, k_ref, vt_ref, ga_ref, gb_ref):
    u = _rms(x_ref[...], g_ref[...]).astype(BF16)

    def proj(seg):
        return jnp.dot(u, w_ref[:, seg * D_MODEL:(seg + 1) * D_MODEL], preferred_element_type=F32)

    reps = D_MODEL // LANES
    cos = jnp.concatenate([cos_ref[...]] * reps, axis=1)
    sin = jnp.concatenate([sin_ref[...]] * reps, axis=1)
    lane = lax.broadcasted_iota(jnp.int32, cos.shape, 1)
    first_half = (lane % HEAD_DIM) < (HEAD_DIM // 2)

    def rope(z):
        swapped = jnp.where(first_half,
                            pltpu.roll(z, D_MODEL - HEAD_DIM // 2, 1),
                            pltpu.roll(z, HEAD_DIM // 2, 1))
        return z * cos + swapped * sin

    xr_ref[...] = proj(0).astype(BF16)
    gr_ref[...] = proj(1).astype(BF16)
    q_ref[...] = (rope(proj(2)) * (HEAD_DIM ** -0.5 * LOG2E)).astype(BF16)
    k_ref[...] = rope(proj(3)).astype(BF16)
    ga_ref[...] = proj(4).astype(BF16)
    gb_ref[...] = proj(5).astype(BF16)
    vt = lax.dot_general(wvt_ref[...], u, (((1,), (1,)), ((), ())), preferred_element_type=F32)
    vt_ref[...] = vt.astype(BF16)


def _inproj(x2d, g_mix, cos_t, sin_t, w_rest, w_vt, seq):
    T = x2d.shape[0]
    tm = _tile(seq, 512)
    n_pos = seq // tm
    tok = pl.BlockSpec((tm, D_MODEL), lambda i: (i, 0))
    pos = pl.BlockSpec((tm, LANES), lambda i: (i % n_pos, 0))
    const = lambda shape: pl.BlockSpec(shape, lambda i: (0,) * len(shape), pipeline_mode=pl.Buffered(1))
    outs = [jax.ShapeDtypeStruct((T, D_MODEL), BF16)] * 4
    out_shape = (outs[0], outs[1], outs[2], outs[3],
                 jax.ShapeDtypeStruct((D_MODEL, T), BF16), outs[0], outs[0])
    return pl.pallas_call(
        _inproj_kernel,
        grid=(T // tm,),
        in_specs=[tok, const((1, D_MODEL)), pos, pos,
                  const(w_rest.shape), const(w_vt.shape)],
        out_specs=(tok, tok, tok, tok, pl.BlockSpec((D_MODEL, tm), lambda i: (0, i)), tok, tok),
        out_shape=out_shape,
        compiler_params=_cparams(("parallel",)),
        name="inproj",
    )(x2d, g_mix, cos_t, sin_t, w_rest, w_vt)


def _lru_kernel(xm_ref, xp_ref, xn_ref, cw_ref, cb_ref, wg_ref, bg_ref, lam_ref,
                h_ref, a_s, b_s, carry_s, *, reverse, tm, n_tiles):
    i = pl.program_id(1)
    ti = (n_tiles - 1 - i) if reverse else i
    C = D_MODEL

    @pl.when(i == 0)
    def _():
        carry_s[...] = jnp.zeros_like(carry_s)

    xm = xm_ref[0].astype(F32)
    xp = xp_ref[0].astype(F32) * (ti > 0).astype(F32)
    xn = xn_ref[0].astype(F32) * (ti < n_tiles - 1).astype(F32)
    big = jnp.concatenate([xp, xm, xn], axis=0)
    xc = jnp.broadcast_to(cb_ref[...], (tm, C))
    n_big = tm + 2 * BF16_ROWS
    for j in range(CONV_W):
        shift = (CONV_LEFT - j) % n_big
        shifted = big if shift == 0 else pltpu.roll(big, shift, 0)
        xc = xc + shifted[BF16_ROWS:BF16_ROWS + tm, :] * cw_ref[j:j + 1, :]

    xcb = xc.astype(BF16)
    r_parts, i_parts = [], []
    for g in range(C // LRU_GROUP):
        z = jnp.dot(xcb[:, g * LRU_GROUP:(g + 1) * LRU_GROUP], wg_ref[g], preferred_element_type=F32)
        z = z + bg_ref[g]
        r_parts.append(z[:, :LRU_GROUP])
        i_parts.append(z[:, LRU_GROUP:])
    r = _sigmoid(jnp.concatenate(r_parts, axis=1))
    ig = _sigmoid(jnp.concatenate(i_parts, axis=1))

    neg_lam = -lam_ref[...]
    softplus = jnp.maximum(neg_lam, 0.0) + jnp.log1p(jnp.exp(-jnp.abs(neg_lam)))
    a = jnp.exp((-LRU_C * softplus) * r)
    y = 1.0 - a * a
    a_s[...] = a
    b_s[...] = (y * lax.rsqrt(jnp.maximum(y, F32_TINY))) * (ig * xc)

    n_slab = tm // SUBLANES
    row = lax.broadcasted_iota(jnp.int32, (SUBLANES, C), 0)

    def body(s, carry):
        sl = (n_slab - 1 - s) if reverse else s
        r0 = pl.multiple_of(sl * SUBLANES, SUBLANES)
        a8 = a_s[pl.ds(r0, SUBLANES), :]
        b8 = b_s[pl.ds(r0, SUBLANES), :]
        for d in (1, 2, 4):
            if reverse:
                valid = row < SUBLANES - d
                shift = SUBLANES - d
            else:
                valid = row >= d
                shift = d
            a_sh = jnp.where(valid, pltpu.roll(a8, shift, 0), 1.0)
            b_sh = jnp.where(valid, pltpu.roll(b8, shift, 0), 0.0)
            b8 = a8 * b_sh + b8
            a8 = a8 * a_sh
        h8 = b8 + a8 * carry
        b_s[pl.ds(r0, SUBLANES), :] = h8
        edge = h8[0:1, :] if reverse else h8[SUBLANES - 1:SUBLANES, :]
        return jnp.broadcast_to(edge, (SUBLANES, C))

    carry_s[...] = lax.fori_loop(0, n_slab, body, carry_s[...], unroll=4)
    h_ref[0] = b_s[...].astype(BF16)


def _lru(xr3, conv_w, conv_b, wg, bg, lam, reverse):
    B, S, C = xr3.shape
    tm = _tile(S, 512)
    n_tiles = S // tm
    hb = tm // BF16_ROWS
    n_halo = S // BF16_ROWS
    tidx = (lambda i: n_tiles - 1 - i) if reverse else (lambda i: i)
    const = lambda shape: pl.BlockSpec(shape, lambda b, i: (0,) * len(shape))
    kern = functools.partial(_lru_kernel, reverse=reverse, tm=tm, n_tiles=n_tiles)
    return pl.pallas_call(
        kern,
        grid=(B, n_tiles),
        in_specs=[
            pl.BlockSpec((1, tm, C), lambda b, i: (b, tidx(i), 0)),
            pl.BlockSpec((1, BF16_ROWS, C), lambda b, i: (b, jnp.maximum(tidx(i) * hb - 1, 0), 0)),
            pl.BlockSpec((1, BF16_ROWS, C), lambda b, i: (b, jnp.minimum((tidx(i) + 1) * hb, n_halo - 1), 0)),
            const(conv_w.shape), const(conv_b.shape), const(wg.shape), const(bg.shape), const(lam.shape),
        ],
        out_specs=pl.BlockSpec((1, tm, C), lambda b, i: (b, tidx(i), 0)),
        out_shape=jax.ShapeDtypeStruct((B, S, C), BF16),
        scratch_shapes=[pltpu.VMEM((tm, C), F32), pltpu.VMEM((tm, C), F32), pltpu.VMEM((SUBLANES, C), F32)],
        compiler_params=_cparams(("parallel", "arbitrary")),
        name="lru_bwd" if reverse else "lru_fwd",
    )(xr3, xr3, xr3, conv_w, conv_b, wg, bg, lam)


def _attn_tile(lq1_ref, lk1_ref, lq2_ref, lk2_ref, q, k_ref, vt_ref, g_ref,
               store, qz_s, s_s, m_s, l_s, acc_s, *, tq, tk, seq, unroll):
    qt = q.astype(F32).T
    chan = lax.broadcasted_iota(jnp.int32, qt.shape, 0)
    qz_s[:, 0:tq] = jnp.where(chan < HEAD_DIM, qt, 0.0).astype(BF16)
    qz_s[:, tq:2 * tq] = jnp.where(chan >= HEAD_DIM, qt, 0.0).astype(BF16)
    m_s[...] = jnp.full_like(m_s, SCORE_FLOOR)
    l_s[...] = jnp.zeros_like(l_s)
    acc_s[...] = jnp.zeros_like(acc_s)
    n_chunks = seq // tk

    n_ct = 2 * tq // MXU_N

    def scores(kc, slot, ct):
        cols = slice(ct * MXU_N, (ct + 1) * MXU_N)
        s_s[slot, ct] = jnp.dot(kc, qz_s[:, cols], preferred_element_type=F32)

    def accumulate(vc, slot, ct):
        cols = slice(ct * MXU_N, (ct + 1) * MXU_N)
        s = s_s[slot, ct]
        m_old = m_s[:, cols]
        m_new = jnp.maximum(m_old, jnp.max(s, axis=0, keepdims=True))
        alpha = jnp.exp2(m_old - m_new)
        p = jnp.exp2(s - m_new)
        pv = jnp.dot(vc, p.astype(BF16), preferred_element_type=F32)
        l_s[:, cols] = alpha * l_s[:, cols] + jnp.sum(p, axis=0, keepdims=True)
        acc_s[:, cols] = alpha * acc_s[:, cols] + pv
        m_s[:, cols] = m_new

    def keys(j):
        return k_ref[0, pl.ds(pl.multiple_of(j * tk, tk), tk), :]

    def values(j):
        return vt_ref[:, pl.ds(pl.multiple_of(j * tk, tk), tk)]

    kc = keys(0)
    for ct in range(n_ct):
        scores(kc, 0, ct)

    def body(jj, c):
        for u in range(unroll):
            j = unroll * jj + u
            kc, va = keys(jnp.minimum(j + 1, n_chunks - 1)), values(j)
            for ct in range(n_ct):
                scores(kc, (u + 1) % 2, ct)
                accumulate(va, u % 2, ct)
        return c

    lax.fori_loop(0, n_chunks // unroll, body, 0)

    lam = (jnp.exp(jnp.sum(lq1_ref[...] * lk1_ref[...], axis=1, keepdims=True))
           - jnp.exp(jnp.sum(lq2_ref[...] * lk2_ref[...], axis=1, keepdims=True)) + LAM_INIT)
    o_all = acc_s[...] / l_s[...]
    o = o_all[:, :tq] - lam * o_all[:, tq:]
    y = o * lax.rsqrt(jnp.mean(o * o, axis=0, keepdims=True) + EPS) * g_ref[...] * (1.0 - LAM_INIT)
    store(y.T.astype(BF16))


def _attn_kernel(lq1_ref, lk1_ref, lq2_ref, lk2_ref, q_ref, k_ref, vt_ref, g_ref, o_ref, *scratch,
                 tq, n_sub, **tile_args):
    for sub in range(n_sub):
        rows = slice(sub * tq, (sub + 1) * tq)

        def store(y, rows=rows):
            o_ref[0, rows, :] = y

        _attn_tile(lq1_ref, lk1_ref, lq2_ref, lk2_ref, q_ref[0, rows, :], k_ref, vt_ref, g_ref,
                   store, *scratch, tq=tq, **tile_args)


def _attn(q3, k3, vt, lam_vecs, subln_g):
    B, S, _ = q3.shape
    tq = _tile(S, 1024)
    tk = _tile(S, 512)
    n_sub = 2 if S % (2 * tq) == 0 else 1
    unroll = max(u for u in (2, 4, 8) if (S // tk) % u == 0)
    vec = pl.BlockSpec((1, HEAD_DIM), lambda b, h, i: (0, 0))
    kern = functools.partial(_attn_kernel, tq=tq, n_sub=n_sub, tk=tk, seq=S, unroll=unroll)
    return pl.pallas_call(
        kern,
        grid=(B, N_HEADS, S // (n_sub * tq)),
        in_specs=[vec, vec, vec, vec,
                  pl.BlockSpec((1, n_sub * tq, V_DIM), lambda b, h, i: (b, i, h)),
                  pl.BlockSpec((1, S, V_DIM), lambda b, h, i: (b, 0, h)),
                  pl.BlockSpec((V_DIM, S), lambda b, h, i: (h, b)),
                  pl.BlockSpec((V_DIM, 1), lambda b, h, i: (0, 0))],
        out_specs=pl.BlockSpec((1, n_sub * tq, V_DIM), lambda b, h, i: (b, i, h)),
        out_shape=jax.ShapeDtypeStruct((B, S, N_HEADS * V_DIM), BF16),
        scratch_shapes=[pltpu.VMEM((V_DIM, 2 * tq), BF16), pltpu.VMEM((2, 2 * tq // MXU_N, tk, MXU_N), F32),
                        pltpu.VMEM((1, 2 * tq), F32), pltpu.VMEM((1, 2 * tq), F32),
                        pltpu.VMEM((V_DIM, 2 * tq), F32)],
        compiler_params=_cparams(("parallel", "parallel", "arbitrary")),
        name="attn",
    )(*lam_vecs, q3, k3, vt, subln_g)


def _gelu_tanh(x):
    return 0.5 * x * (1.0 + jnp.tanh(math.sqrt(2.0 / math.pi) * (x + 0.044715 * (x * x * x))))


def _merge_kernel(x_ref, hf_ref, hb_ref, gr_ref, ga_ref, gb_ref, yb_ref, wout_ref, g_ref,
                  wrh_ref, wrl_ref, br_ref, h1_ref, xn_ref, lt_ref):
    f = lambda ref: ref[...].astype(F32)
    ya = _gelu_tanh(f(gr_ref)) * (f(hf_ref) + f(hb_ref))
    mixed = _sigmoid(f(ga_ref)) * ya + _sigmoid(f(gb_ref)) * f(yb_ref)
    h1 = x_ref[...] + jnp.dot(mixed.astype(BF16), wout_ref[...], preferred_element_type=F32)
    h1_ref[...] = h1
    xn = _rms(h1, g_ref[...])
    _store_tile_rows(xn_ref, xn)
    xn_hi = xn.astype(BF16)
    xn_lo = (xn - xn_hi.astype(F32)).astype(BF16)
    nt = lambda w, v: lax.dot_general(w, v, (((1,), (1,)), ((), ())), preferred_element_type=F32)
    lt_ref[...] = nt(wrh_ref[...], xn_hi) + nt(wrh_ref[...], xn_lo) + nt(wrl_ref[...], xn_hi) + br_ref[...]


def _merge(x2d, hf, hb, gr, ga, gb, yb, w_out, g_ffn, wr_hi, wr_lo, b_router):
    T = x2d.shape[0]
    tm = _tile(T, 512)
    tok = pl.BlockSpec((tm, D_MODEL), lambda i: (i, 0))
    const = lambda shape: pl.BlockSpec(shape, lambda i: (0,) * len(shape))
    return pl.pallas_call(
        _merge_kernel,
        grid=(T // tm,),
        in_specs=[tok] * 7 + [const(w_out.shape), const(g_ffn.shape), const(wr_hi.shape),
                              const(wr_lo.shape), const(b_router.shape)],
        out_specs=(tok, _tile_rows_spec(tm, lambda i: (i, 0)), pl.BlockSpec((N_EXPERTS, tm), lambda i: (0, i))),
        out_shape=(jax.ShapeDtypeStruct((T, D_MODEL), F32), _tile_rows_shape(T),
                   jax.ShapeDtypeStruct((N_EXPERTS, T), F32)),
        compiler_params=_cparams(("parallel",)),
        name="merge",
    )(x2d, hf, hb, gr, ga, gb, yb, w_out, g_ffn, wr_hi, wr_lo, b_router)


def _route_kernel(lt_ref, tri_ref, idx_ref, gate_ref, rank_ref, cnt_ref, carry_s, *, tn):
    i = pl.program_id(0)

    @pl.when(i == 0)
    def _():
        carry_s[...] = jnp.zeros_like(carry_s)

    work = lt_ref[...]
    eidx = lax.broadcasted_iota(jnp.int32, work.shape, 0)
    vals, idxs = [], []
    for _ in range(TOP_K):
        mx = jnp.max(work, axis=0, keepdims=True)
        ix = jnp.min(jnp.where(work == mx, eidx, N_EXPERTS), axis=0, keepdims=True)
        vals.append(mx)
        idxs.append(ix)
        work = jnp.where(eidx == ix, -jnp.inf, work)
    ex = [jnp.exp(v - vals[0]) for v in vals]
    denom = ex[0] + ex[1] + ex[2] + ex[3]
    sel = jnp.zeros(work.shape, F32)
    for ix in idxs:
        sel = sel + (eidx == ix).astype(F32)
    before = jnp.dot(sel.astype(BF16), tri_ref[...], preferred_element_type=F32) + carry_s[:, 0:1]
    for k in range(TOP_K):
        rank = jnp.sum(jnp.where(eidx == idxs[k], before, 0.0), axis=0, keepdims=True)
        rank_ref[k:k + 1, :] = rank.astype(jnp.int32)
        idx_ref[k:k + 1, :] = idxs[k]
        gate_ref[k:k + 1, :] = ex[k] / denom
    carry_s[...] = carry_s[...] + jnp.sum(sel, axis=1, keepdims=True)
    cnt_ref[...] = carry_s[...].astype(jnp.int32)


def _route(logits_t):
    E, T = logits_t.shape
    tn = _tile(T, 1024)
    tri = (lax.broadcasted_iota(jnp.int32, (tn, tn), 0) < lax.broadcasted_iota(jnp.int32, (tn, tn), 1)).astype(BF16)
    kt = pl.BlockSpec((TOP_K, tn), lambda i: (0, i))
    kern = functools.partial(_route_kernel, tn=tn)
    return pl.pallas_call(
        kern,
        grid=(T // tn,),
        in_specs=[pl.BlockSpec((E, tn), lambda i: (0, i)), pl.BlockSpec((tn, tn), lambda i: (0, 0))],
        out_specs=(kt, kt, kt, pl.BlockSpec((E, LANES), lambda i: (0, 0))),
        out_shape=(jax.ShapeDtypeStruct((TOP_K, T), jnp.int32), jax.ShapeDtypeStruct((TOP_K, T), F32),
                   jax.ShapeDtypeStruct((TOP_K, T), jnp.int32), jax.ShapeDtypeStruct((E, LANES), jnp.int32)),
        scratch_shapes=[pltpu.VMEM((E, LANES), F32)],
        compiler_params=_cparams(("arbitrary",)),
        name="route",
    )(logits_t, tri)


def _row_copy(src, src_row, dst, dst_row, sem):
    tile = lambda r: pl.ds(pl.multiple_of(r * ROW_TILES, ROW_TILES), ROW_TILES)
    return pltpu.make_async_copy(src.at[tile(src_row)], dst.at[tile(dst_row)], sem)


def _wait_rows(src, dst, n_rows, sem):
    n = pl.multiple_of(n_rows * ROW_TILES, ROW_TILES)
    pltpu.make_async_copy(src.at[pl.ds(0, n)], dst.at[pl.ds(0, n)], sem).wait()


def _scatter_kernel(fill_lo_ref, fill_hi_ref, dest_ref, xn_ref, xs_ref, zero_s, sem, zsem, *, tt):
    i = pl.program_id(0)

    @pl.when(i == 0)
    def _():
        zero_s[...] = jnp.zeros_like(zero_s)

        def per_expert(e, n):
            lo, hi = fill_lo_ref[e], fill_hi_ref[e]

            def fill(r, c):
                _row_copy(zero_s, 0, xs_ref, r, zsem).start()
                return c

            lax.fori_loop(lo, hi, fill, 0)
            return n + (hi - lo)

        n_fill = lax.fori_loop(0, N_EXPERTS, per_expert, 0)

        @pl.when(n_fill > 0)
        def _():
            _wait_rows(xs_ref, xs_ref, n_fill, zsem)

    def body(t, c):
        for k in range(TOP_K):
            _row_copy(xn_ref, t, xs_ref, dest_ref[t * TOP_K + k], sem).start(priority=k % 2)
        return c

    lax.fori_loop(0, tt, body, 0, unroll=2)
    _wait_rows(xs_ref, xs_ref, TOP_K * tt, sem)


def _scatter_rows(fill_lo, fill_hi, dest_flat, xn, n_rows):
    T = xn.shape[0] // ROW_TILES
    tt = _tile(T, 2048)
    kern = functools.partial(_scatter_kernel, tt=tt)
    return pl.pallas_call(
        kern,
        grid_spec=pltpu.PrefetchScalarGridSpec(
            num_scalar_prefetch=2,
            grid=(T // tt,),
            in_specs=[pl.BlockSpec((TOP_K * tt,), lambda i, lo, hi: (i,), memory_space=pltpu.SMEM),
                      _tile_rows_spec(tt, lambda i, lo, hi: (i, 0))],
            out_specs=pl.BlockSpec(memory_space=pl.ANY),
            scratch_shapes=[pltpu.VMEM((ROW_TILES, LANES), F32), pltpu.SemaphoreType.DMA(()),
                            pltpu.SemaphoreType.DMA(())],
        ),
        out_shape=_tile_rows_shape(n_rows),
        compiler_params=_cparams(("arbitrary",)),
        name="scatter",
    )(fill_lo, fill_hi, dest_flat, xn)


def _split_w1_kernel(w_ref, perm_ref, wg_ref, wl_ref):
    half = MXU_N // 2
    for g in range(w_ref.shape[2] // MXU_N):
        z = jnp.dot(w_ref[0, :, g * MXU_N:(g + 1) * MXU_N].astype(BF16), perm_ref[...],
                    preferred_element_type=F32)
        wg_ref[0, :, g * half:(g + 1) * half] = z[:, :half].astype(BF16)
        wl_ref[0, :, g * half:(g + 1) * half] = z[:, half:].astype(BF16)


def _split_w1(w1):
    E, D, D2 = w1.shape
    tr = _tile(D, 512)
    col = lax.broadcasted_iota(jnp.int32, (MXU_N, MXU_N), 1)
    row = lax.broadcasted_iota(jnp.int32, (MXU_N, MXU_N), 0)
    perm = (row == jnp.where(col < MXU_N // 2, 2 * col, 2 * (col - MXU_N // 2) + 1)).astype(BF16)
    out = pl.BlockSpec((1, tr, D2 // 2), lambda e, i: (e, i, 0))
    return pl.pallas_call(
        _split_w1_kernel,
        grid=(E, D // tr),
        in_specs=[pl.BlockSpec((1, tr, D2), lambda e, i: (e, i, 0)),
                  pl.BlockSpec((MXU_N, MXU_N), lambda e, i: (0, 0))],
        out_specs=(out, out),
        out_shape=(jax.ShapeDtypeStruct((E, D, D2 // 2), BF16),) * 2,
        compiler_params=_cparams(("parallel", "parallel")),
        name="split_w1",
    )(w1, perm)


def _ffn_kernel(be_ref, nu_ref, xs_ref, w1g_ref, w1l_ref, b1g_ref, b1l_ref, w2_ref, b2_ref, ys_ref):
    i = pl.program_id(0)

    @pl.when(i < nu_ref[0])
    def _():
        x = _load_tile_rows(xs_ref, 0, ROWS_PER_BLOCK).astype(BF16)
        hg = jnp.dot(x, w1g_ref[0], preferred_element_type=F32) + b1g_ref[0]
        hl = jnp.dot(x, w1l_ref[0], preferred_element_type=F32) + b1l_ref[0]
        xg = jnp.minimum(hg, SWIGLU_LIMIT)
        xl = jnp.clip(hl, -SWIGLU_LIMIT, SWIGLU_LIMIT)
        act = xg * _sigmoid(SWIGLU_ALPHA * xg) * (xl + 1.0)
        _store_tile_rows(ys_ref, jnp.dot(act.astype(BF16), w2_ref[0], preferred_element_type=F32) + b2_ref[0])

    @pl.when(i >= nu_ref[0])
    def _():
        ys_ref[...] = jnp.zeros_like(ys_ref)


def _ffn(block_e, n_used, xs, w1g, w1l, b1g, b1l, w2, b2):
    n_rows = xs.shape[0] // ROW_TILES
    n_blocks = n_rows // ROWS_PER_BLOCK
    rows = _tile_rows_spec(ROWS_PER_BLOCK, lambda i, be, nu: (i, 0))
    rows_in = _tile_rows_spec(ROWS_PER_BLOCK, lambda i, be, nu: (jnp.minimum(i, nu[0] - 1), 0))
    wspec = pl.BlockSpec((1, D_MODEL, D_MODEL), lambda i, be, nu: (be[i], 0, 0))
    bspec = pl.BlockSpec((1, 1, D_MODEL), lambda i, be, nu: (be[i], 0, 0))
    return pl.pallas_call(
        _ffn_kernel,
        grid_spec=pltpu.PrefetchScalarGridSpec(
            num_scalar_prefetch=2,
            grid=(n_blocks,),
            in_specs=[rows_in, wspec, wspec, bspec, bspec, wspec, bspec],
            out_specs=rows,
        ),
        out_shape=_tile_rows_shape(n_rows),
        compiler_params=_cparams(("arbitrary",)),
        name="ffn",
    )(block_e, n_used, xs, w1g, w1l, b1g, b1l, w2, b2)


def _final_kernel(dest_ref, next_ref, h1_ref, ys_ref, gate_ref, p_ref, gpl_ref, wpg_ref, wpe_ref, gfin_ref,
                  out_ref, ybuf_a, ybuf_b, sems, *, tm, n_tiles):
    i = pl.program_id(0)
    half = tm // 2

    def issue(idx_ref, h, buf):
        for t in range(half):
            for k in range(TOP_K):
                _row_copy(ys_ref, idx_ref[(h * half + t) * TOP_K + k], buf, k * half + t,
                          sems.at[h]).start(priority=k % 2)

    def combine(h, buf):
        rows = pl.ds(h * half, half)
        h2 = h1_ref[rows, :]
        for k in range(TOP_K):
            h2 = h2 + gate_ref[rows, k:k + 1] * _load_tile_rows(buf, k * half, half)
        hn = _rms(h2, gpl_ref[...]).astype(BF16)
        gate = _sigmoid(jnp.dot(hn, wpg_ref[...], preferred_element_type=F32))
        pe = jnp.dot(p_ref[rows, :].astype(BF16), wpe_ref[...], preferred_element_type=F32)
        out_ref[rows, :] = _rms(h2 + gate * pe, gfin_ref[...])

    @pl.when(i == 0)
    def _():
        issue(dest_ref, 0, ybuf_a)

    _wait_rows(ys_ref, ybuf_a, TOP_K * half, sems.at[0])
    issue(dest_ref, 1, ybuf_b)
    combine(0, ybuf_a)
    _wait_rows(ys_ref, ybuf_b, TOP_K * half, sems.at[1])

    @pl.when(i < n_tiles - 1)
    def _():
        issue(next_ref, 0, ybuf_a)
        combine(1, ybuf_b)

    @pl.when(i == n_tiles - 1)
    def _():
        combine(1, ybuf_b)


def _final(dest_flat, h1, ys, gates_tk, p2d, g_pl, w_pg, w_pe, g_final):
    T = h1.shape[0]
    tm = _tile(T, 256)
    n_tiles = T // tm
    tok = pl.BlockSpec((tm, D_MODEL), lambda i: (i, 0))
    const = lambda shape: pl.BlockSpec(shape, lambda i: (0,) * len(shape))
    kern = functools.partial(_final_kernel, tm=tm, n_tiles=n_tiles)
    half_buf = pltpu.VMEM((TOP_K * (tm // 2) * ROW_TILES, LANES), F32)
    return pl.pallas_call(
        kern,
        grid=(n_tiles,),
        in_specs=[pl.BlockSpec((TOP_K * tm,), lambda i: (i,), memory_space=pltpu.SMEM),
                  pl.BlockSpec((TOP_K * tm,), lambda i: (jnp.minimum(i + 1, n_tiles - 1),),
                               memory_space=pltpu.SMEM),
                  tok, pl.BlockSpec(memory_space=pl.ANY),
                  pl.BlockSpec((tm, TOP_K), lambda i: (i, 0)), pl.BlockSpec((tm, PLE_DIM), lambda i: (i, 0)),
                  const(g_pl.shape), const(w_pg.shape), const(w_pe.shape), const(g_final.shape)],
        out_specs=tok,
        out_shape=jax.ShapeDtypeStruct((T, D_MODEL), F32),
        scratch_shapes=[half_buf, half_buf, pltpu.SemaphoreType.DMA((2,))],
        compiler_params=_cparams(("arbitrary",)),
        name="final",
    )(dest_flat, dest_flat, h1, ys, gates_tk, p2d, g_pl, w_pg, w_pe, g_final)


def _rope_tables(S):
    pos = jnp.arange(S, dtype=F32)
    inv = ROPE_THETA ** (-jnp.arange(0, HEAD_DIM, 2, dtype=F32) / HEAD_DIM)
    ang = pos[:, None] * inv[None, :]
    ang = jnp.concatenate([ang, ang, ang, ang], axis=-1)
    sign = jnp.where((jnp.arange(LANES) % HEAD_DIM) < HEAD_DIM // 2, -1.0, 1.0).astype(F32)
    return jnp.cos(ang), jnp.sin(ang) * sign


def _block_diag_groups(w):
    per = LRU_GROUP // LRU_BLOCK
    g = w.shape[0] // per
    eye = jnp.eye(per, dtype=w.dtype)
    return jnp.einsum('gbij,bc->gbicj', w.reshape(g, per, LRU_BLOCK, LRU_BLOCK), eye).reshape(g, LRU_GROUP, LRU_GROUP)


def _prep_weights(g_mix, w_in, conv_w, conv_b, lru_wa, lru_ba, lru_wx, lru_bx, lru_lambda,
                  lam_q1, lam_k1, lam_q2, lam_k2, subln_g, w_out, g_ffn, w_router, b_router,
                  w1, b1, w2, b2, g_pl, w_pg, w_pe, g_final):
    l = 0
    W = D_MODEL
    w = w_in[l]
    pw = dict(
        g_mix=g_mix[l][None, :],
        w_rest=jnp.concatenate([w[:, :4 * W], w[:, 5 * W:]], axis=1).astype(BF16),
        w_vt=w[:, 4 * W:5 * W].T.astype(BF16),
        conv_w=conv_w[l], conv_b=conv_b[l][None, :],
        lam_vecs=tuple(v[l][None, :] for v in (lam_q1, lam_k1, lam_q2, lam_k2)),
        subln_g=subln_g[l][:, None],
        w_out=w_out[l].astype(BF16), g_ffn=g_ffn[l][None, :],
        b_router=b_router[l][:, None],
        b1g=b1[l][:, None, 0::2], b1l=b1[l][:, None, 1::2],
        w2=w2[l].astype(BF16), b2=b2[l][:, None, :],
        g_pl=g_pl[l][None, :], w_pg=w_pg[l].astype(BF16), w_pe=w_pe[l].astype(BF16),
        g_final=g_final[None, :],
    )
    pw['w1g'], pw['w1l'] = _split_w1(w1[l])
    wr_t = w_router[l].T
    wr_hi = wr_t.astype(BF16)
    pw['wr_hi'] = wr_hi
    pw['wr_lo'] = (wr_t - wr_hi.astype(F32)).astype(BF16)
    for d in range(2):
        wa = _block_diag_groups(lru_wa[l, d])
        wx = _block_diag_groups(lru_wx[l, d])
        pw[f'lru_wg{d}'] = jnp.concatenate([wa, wx], axis=2).astype(BF16)
        ng = D_MODEL // LRU_GROUP
        pw[f'lru_bg{d}'] = jnp.concatenate([lru_ba[l, d].reshape(ng, 1, LRU_GROUP),
                                            lru_bx[l, d].reshape(ng, 1, LRU_GROUP)], axis=2)
        pw[f'lru_lam{d}'] = lru_lambda[l, d][None, :]
    return pw


def _moe_plan(idx, rank, counts, T):
    padded = ((counts + ROWS_PER_BLOCK - 1) // ROWS_PER_BLOCK) * ROWS_PER_BLOCK
    pad_end = jnp.cumsum(padded)
    pad_start = pad_end - padded
    onehot = idx[..., None] == jnp.arange(N_EXPERTS, dtype=idx.dtype)
    dest = jnp.sum(jnp.where(onehot, pad_start.astype(idx.dtype), 0), axis=-1) + rank
    n_blocks = -(-(T * TOP_K) // ROWS_PER_BLOCK) + N_EXPERTS
    block_start = jnp.arange(n_blocks, dtype=pad_end.dtype) * ROWS_PER_BLOCK
    block_e = jnp.minimum(jnp.sum(pad_end[None, :] <= block_start[:, None], axis=1),
                          N_EXPERTS - 1).astype(jnp.int32)
    n_used = (pad_end[-1:] // ROWS_PER_BLOCK).astype(jnp.int32)
    dest_flat = dest.T.reshape(-1)
    n_rows = n_blocks * ROWS_PER_BLOCK
    fill_lo = (pad_start + counts).astype(jnp.int32)
    fill_hi = pad_end.astype(jnp.int32).at[N_EXPERTS - 1].set(n_rows)
    return dest_flat, fill_lo, fill_hi, block_e, n_used, n_rows


def _trunk(x, p, pw, cos_t, sin_t):
    B, S, _ = x.shape
    T = B * S
    x2d = x.reshape(T, D_MODEL)
    xr, gr, q, k, vt, ga, gb = _inproj(x2d, pw['g_mix'], cos_t, sin_t, pw['w_rest'], pw['w_vt'], S)
    xr3 = xr.reshape(B, S, D_MODEL)
    hf = _lru(xr3, pw['conv_w'], pw['conv_b'], pw['lru_wg0'], pw['lru_bg0'], pw['lru_lam0'], False)
    hb = _lru(xr3, pw['conv_w'], pw['conv_b'], pw['lru_wg1'], pw['lru_bg1'], pw['lru_lam1'], True)
    yb = _attn(q.reshape(B, S, D_MODEL), k.reshape(B, S, D_MODEL), vt, pw['lam_vecs'], pw['subln_g'])
    h1, xn, logits_t = _merge(x2d, hf.reshape(T, D_MODEL), hb.reshape(T, D_MODEL), gr, ga, gb,
                              yb.reshape(T, D_MODEL), pw['w_out'], pw['g_ffn'],
                              pw['wr_hi'], pw['wr_lo'], pw['b_router'])
    idx, gates, rank, cnt = _route(logits_t)
    dest_flat, fill_lo, fill_hi, block_e, n_used, n_rows = _moe_plan(idx, rank, cnt[:, 0], T)
    xs = _scatter_rows(fill_lo, fill_hi, dest_flat, xn, n_rows)
    ys = _ffn(block_e, n_used, xs, pw['w1g'], pw['w1l'], pw['b1g'], pw['b1l'], pw['w2'], pw['b2'])
    out = _final(dest_flat, h1, ys, gates.T, p[0].reshape(T, PLE_DIM),
                 pw['g_pl'], pw['w_pg'], pw['w_pe'], pw['g_final'])
    return out.reshape(B, S, D_MODEL)


def kernel(x_prompt, x_sample, p_prompt, p_sample, g_mix, w_in, conv_w, conv_b, lru_wa, lru_ba, lru_wx, lru_bx, lru_lambda, lam_q1, lam_k1, lam_q2, lam_k2, subln_g, w_out, g_ffn, w_router, b_router, w1, b1, w2, b2, g_pl, w_pg, w_pe, g_final):
    pw = _prep_weights(g_mix, w_in, conv_w, conv_b, lru_wa, lru_ba, lru_wx, lru_bx, lru_lambda,
                       lam_q1, lam_k1, lam_q2, lam_k2, subln_g, w_out, g_ffn, w_router, b_router,
                       w1, b1, w2, b2, g_pl, w_pg, w_pe, g_final)
    cos_t, sin_t = _rope_tables(max(x_prompt.shape[1], x_sample.shape[1]))
    return (_trunk(x_prompt, p_prompt, pw, cos_t, sin_t), _trunk(x_sample, p_sample, pw, cos_t, sin_t))
```
